```python
import jax, jax.numpy as jnp
from jax import lax
import numpy as np

D_MODEL = 1024
BATCH = 4
SEQ = 8192
DEPTH = 4

GRID_W = 64
CTX_LEN = 256
MOD_CHUNKS = 9
D_FF = 2816
EPS = 1e-6

LRU_WIDTH = 384
LRU_BLOCKS = 6
LRU_BLOCK_W = LRU_WIDTH // LRU_BLOCKS
LRU_C = 8.0
CONV_W = 4
CONV_LEFT = 2

MLA_HEADS = 6
MLA_Q_RANK = 256
MLA_KV_RANK = 128
MLA_NOPE = 64
MLA_ROPE = 32
MLA_QK_DIM = MLA_NOPE + MLA_ROPE
MLA_V = 64
ROPE_BASE = 10000.0
ATTN_BLOCK = 128

NA_HEADS = 4
NA_HEAD_DIM = 64
NA_WIDTH = NA_HEADS * NA_HEAD_DIM
NA_WIN_ROWS = 8
NA_WIN_COLS = 16
NA_QCOL_BLOCK = 16
NA_KCOL_BLOCK = 32

MIX_WIDTH = LRU_WIDTH + MLA_HEADS * MLA_V + NA_WIDTH
IN_SPLITS = (LRU_WIDTH, LRU_WIDTH, MLA_Q_RANK, MLA_KV_RANK, MLA_ROPE, NA_WIDTH, NA_WIDTH, NA_WIDTH)
IN_WIDTH = 1952

kernel_name = 'hybrid_lru_mla_natten_macaron_dit'


def rms_norm(x, g):
    xf = x.astype(jnp.float32)
    y = xf * lax.rsqrt(jnp.mean(xf * xf, axis=-1, keepdims=True) + EPS)
    return (y * g.astype(jnp.float32)).astype(x.dtype)


def modulate(x, shift, scale):
    return x * (1 + scale[:, None, :]) + shift[:, None, :]


def swiglu(x, w_in, w_out):
    gate, up = jnp.split(x @ w_in, 2, axis=-1)
    return (jax.nn.silu(gate) * up) @ w_out


def split_columns(y):
    parts, start = [], 0
    for width in IN_SPLITS:
        parts.append(y[..., start:start + width])
        start += width
    return parts


def attention(q, k, v):
    s = jnp.einsum('bhqd,bhkd->bhqk', q, k).astype(jnp.float32) * (q.shape[-1] ** -0.5)
    p = jax.nn.softmax(s, axis=-1).astype(v.dtype)
    return jnp.einsum('bhqk,bhkd->bhqd', p, v)


def blocked_attention(q, k, v):
    B, H, T, _ = q.shape
    nb = T // ATTN_BLOCK
    qb = jnp.moveaxis(q.reshape(B, H, nb, ATTN_BLOCK, q.shape[-1]), 2, 0)
    ob = lax.map(lambda qi: attention(qi, k, v), qb)
    return jnp.moveaxis(ob, 0, 2).reshape(B, H, T, v.shape[-1])


def axial_rope(T, dtype):
    t = jnp.arange(T)
    pos = jnp.stack([t // GRID_W, t % GRID_W], axis=-1).astype(jnp.float32)
    half = MLA_ROPE // 2
    inv = ROPE_BASE ** (-jnp.arange(0, half, 2, dtype=jnp.float32) / half)
    ang = pos[:, None, :, None] * inv
    return jnp.cos(ang).astype(dtype), jnp.sin(ang).astype(dtype)


def apply_rope(x, cos, sin):
    xs = x.reshape(*x.shape[:-1], 2, 2, MLA_ROPE // 4)
    x1, x2 = xs[..., 0, :], xs[..., 1, :]
    out = jnp.stack([x1 * cos - x2 * sin, x2 * cos + x1 * sin], axis=-2)
    return out.reshape(x.shape)


def depthwise_conv(x, w, b):
    T = x.shape[1]
    xp = jnp.pad(x, ((0, 0), (CONV_LEFT, CONV_W - 1 - CONV_LEFT), (0, 0)))
    y = b
    for j in range(CONV_W):
        y = y + xp[:, j:j + T] * w[j]
    return y


def rglru(x, w_a, b_a, w_x, b_x, lam, h0):
    B, T, W = x.shape
    xb = x.reshape(B, T, LRU_BLOCKS, LRU_BLOCK_W)
    r = jax.nn.sigmoid(jnp.einsum('btnc,ncd->btnd', xb, w_a).reshape(B, T, W) + b_a).astype(jnp.float32)
    i = jax.nn.sigmoid(jnp.einsum('btnc,ncd->btnd', xb, w_x).reshape(B, T, W) + b_x)
    log_a = -LRU_C * r * jax.nn.softplus(-lam.astype(jnp.float32))
    a = jnp.exp(log_a)
    u = jnp.sqrt(-jnp.expm1(2.0 * log_a)) * (i * x).astype(jnp.float32)
    u = u.at[:, 0].add(a[:, 0] * h0.astype(jnp.float32))

    def combine(left, right):
        a_l, b_l = left
        a_r, b_r = right
        return a_l * a_r, a_r * b_l + b_r

    _, h = lax.associative_scan(combine, (a, u), axis=1)
    return h.astype(x.dtype)


def bidir_rglru(u_c, u_l, w_a, b_a, w_x, b_x, lam, need_ctx):
    zeros = jnp.zeros((u_c.shape[0], u_c.shape[2]), u_c.dtype)
    hc_f = rglru(u_c, w_a[0], b_a[0], w_x[0], b_x[0], lam[0], zeros)
    hl_f = rglru(u_l, w_a[0], b_a[0], w_x[0], b_x[0], lam[0], hc_f[:, -1])
    hc_b = jnp.flip(rglru(jnp.flip(u_c, 1), w_a[1], b_a[1], w_x[1], b_x[1], lam[1], zeros), 1)
    hl_b = jnp.flip(rglru(jnp.flip(u_l, 1), w_a[1], b_a[1], w_x[1], b_x[1], lam[1], hc_b[:, 0]), 1)
    yc = hc_f + hc_b if need_ctx else None
    return yc, hl_f + hl_b


def mla_queries(cq, q_norm, w_uq, q_gain, cos=None, sin=None):
    B, T, _ = cq.shape
    q = rms_norm((rms_norm(cq, q_norm) @ w_uq).reshape(B, T, MLA_HEADS, MLA_QK_DIM), q_gain)
    if cos is not None:
        q = jnp.concatenate([q[..., :MLA_NOPE], apply_rope(q[..., MLA_NOPE:], cos, sin)], axis=-1)
    return q.transpose(0, 2, 1, 3)


def mla_keys_values(ckv, kr, kv_norm, w_ukv, k_gain, cos=None, sin=None):
    B, T, _ = ckv.shape
    kv = (rms_norm(ckv, kv_norm) @ w_ukv).reshape(B, T, MLA_HEADS, MLA_NOPE + MLA_V)
    k_nope, v = kv[..., :MLA_NOPE], kv[..., MLA_NOPE:]
    k_rope = jnp.broadcast_to(kr[:, :, None, :], (B, T, MLA_HEADS, MLA_ROPE))
    k = rms_norm(jnp.concatenate([k_nope, k_rope], axis=-1), k_gain)
    if cos is not None:
        k = jnp.concatenate([k[..., :MLA_NOPE], apply_rope(k[..., MLA_NOPE:], cos, sin)], axis=-1)
    return k.transpose(0, 2, 1, 3), v.transpose(0, 2, 1, 3)


def neighborhood_attention(q, k, v, k_ctx, v_ctx, rpb):
    B, T, H, dh = q.shape
    rows = T // GRID_W
    wr = min(NA_WIN_ROWS, rows)
    n_cb = GRID_W // NA_QCOL_BLOCK
    n_loc = wr * NA_KCOL_BLOCK
    qg = q.reshape(B, rows, GRID_W, H, dh)
    kg = k.reshape(B, rows, GRID_W, H, dh)
    vg = v.reshape(B, rows, GRID_W, H, dh)
    q_rows = jnp.arange(rows)
    row_start = jnp.clip(q_rows - wr // 2, 0, rows - wr)
    q_cols = jnp.arange(GRID_W).reshape(n_cb, NA_QCOL_BLOCK)
    win_start = jnp.clip(q_cols - NA_WIN_COLS // 2, 0, GRID_W - NA_WIN_COLS)
    kblk_start = jnp.clip(jnp.arange(n_cb) * NA_QCOL_BLOCK - NA_WIN_COLS // 2, 0, GRID_W - NA_KCOL_BLOCK)
    k_cols = kblk_start[:, None] + jnp.arange(NA_KCOL_BLOCK)
    kc = k_cols[:, None, :]
    col_in = (kc >= win_start[..., None]) & (kc < win_start[..., None] + NA_WIN_COLS)
    col_off = jnp.clip(kc - q_cols[..., None] + NA_WIN_COLS - 1, 0, 2 * NA_WIN_COLS - 2)
    mask = jnp.broadcast_to(col_in[:, :, None, :], (n_cb, NA_QCOL_BLOCK, wr, NA_KCOL_BLOCK))
    mask = mask.reshape(n_cb, NA_QCOL_BLOCK, n_loc)
    scale = dh ** -0.5

    def one_row(args):
        q_row, r, r0 = args
        k_band = lax.dynamic_slice_in_dim(kg, r0, wr, axis=1)[:, :, k_cols]
        v_band = lax.dynamic_slice_in_dim(vg, r0, wr, axis=1)[:, :, k_cols]
        k_blk = k_band.transpose(0, 4, 2, 1, 3, 5).reshape(B, H, n_cb, n_loc, dh)
        v_blk = v_band.transpose(0, 4, 2, 1, 3, 5).reshape(B, H, n_cb, n_loc, dh)
        q_blk = q_row.reshape(B, n_cb, NA_QCOL_BLOCK, H, dh).transpose(0, 3, 1, 2, 4)
        row_off = r0 + jnp.arange(wr) - r + NA_WIN_ROWS - 1
        bias = rpb[:, row_off][:, :, col_off]
        bias = bias.transpose(0, 2, 3, 1, 4).reshape(H, n_cb, NA_QCOL_BLOCK, n_loc).astype(jnp.float32)
        s_loc = jnp.einsum('bhnqd,bhnkd->bhnqk', q_blk, k_blk).astype(jnp.float32) * scale + bias
        s_loc = jnp.where(mask, s_loc, -jnp.inf)
        s_ctx = jnp.einsum('bhnqd,bhcd->bhnqc', q_blk, k_ctx).astype(jnp.float32) * scale
        p = jax.nn.softmax(jnp.concatenate([s_loc, s_ctx], axis=-1), axis=-1).astype(v.dtype)
        o = (jnp.einsum('bhnqk,bhnkd->bhnqd', p[..., :n_loc], v_blk)
             + jnp.einsum('bhnqc,bhcd->bhnqd', p[..., n_loc:], v_ctx))
        return o.transpose(0, 2, 3, 1, 4).reshape(B, GRID_W, H, dh)

    out = lax.map(one_row, (jnp.moveaxis(qg, 1, 0), q_rows, row_start))
    return jnp.moveaxis(out, 0, 1).reshape(B, T, H, dh)


def hybrid_mixer(xc, xl, w_in, w_out, conv_w, conv_b, w_a, b_a, w_x, b_x, lam,
                 q_norm, w_uq, kv_norm, w_ukv, mq_gain, mk_gain, nq_gain, nk_gain, rpb,
                 cos, sin, need_ctx):
    B, C, _ = xc.shape
    T = xl.shape[1]
    c_lx, c_lg, c_cq, c_ckv, c_kr, c_nq, c_nk, c_nv = split_columns(xc @ w_in)
    l_lx, l_lg, l_cq, l_ckv, l_kr, l_nq, l_nk, l_nv = split_columns(xl @ w_in)

    yc_lru, yl_lru = bidir_rglru(depthwise_conv(c_lx, conv_w, conv_b), depthwise_conv(l_lx, conv_w, conv_b),
                                 w_a, b_a, w_x, b_x, lam, need_ctx)
    yl_lru = yl_lru * jax.nn.gelu(l_lg)

    kc_m, vc_m = mla_keys_values(c_ckv, c_kr, kv_norm, w_ukv, mk_gain)
    kl_m, vl_m = mla_keys_values(l_ckv, l_kr, kv_norm, w_ukv, mk_gain, cos, sin)
    ql_m = mla_queries(l_cq, q_norm, w_uq, mq_gain, cos, sin)
    yl_mla = blocked_attention(ql_m, jnp.concatenate([kc_m, kl_m], axis=2), jnp.concatenate([vc_m, vl_m], axis=2))
    yl_mla = yl_mla.transpose(0, 2, 1, 3).reshape(B, T, MLA_HEADS * MLA_V)

    kc_n = rms_norm(c_nk.reshape(B, C, NA_HEADS, NA_HEAD_DIM), nk_gain).transpose(0, 2, 1, 3)
    vc_n = c_nv.reshape(B, C, NA_HEADS, NA_HEAD_DIM).transpose(0, 2, 1, 3)
    ql_n = rms_norm(l_nq.reshape(B, T, NA_HEADS, NA_HEAD_DIM), nq_gain)
    kl_n = rms_norm(l_nk.reshape(B, T, NA_HEADS, NA_HEAD_DIM), nk_gain)
    vl_n = l_nv.reshape(B, T, NA_HEADS, NA_HEAD_DIM)
    yl_na = neighborhood_attention(ql_n, kl_n, vl_n, kc_n, vc_n, rpb).reshape(B, T, NA_WIDTH)

    yl = jnp.concatenate([yl_lru, yl_mla, yl_na], axis=-1) @ w_out
    if not need_ctx:
        return None, yl
    yc_lru = yc_lru * jax.nn.gelu(c_lg)
    qc_m = mla_queries(c_cq, q_norm, w_uq, mq_gain)
    yc_mla = attention(qc_m, kc_m, vc_m).transpose(0, 2, 1, 3).reshape(B, C, MLA_HEADS * MLA_V)
    qc_n = rms_norm(c_nq.reshape(B, C, NA_HEADS, NA_HEAD_DIM), nq_gain).transpose(0, 2, 1, 3)
    yc_na = attention(qc_n, kc_n, vc_n).transpose(0, 2, 1, 3).reshape(B, C, NA_WIDTH)
    yc = jnp.concatenate([yc_lru, yc_mla, yc_na], axis=-1) @ w_out
    return yc, yl


def setup_inputs(seed: int = 0) -> dict:
    key = jax.random.key(seed)
    ks = iter(jax.random.split(key, 40))

    def normal(shape, scale):
        return jax.random.normal(next(ks), shape, jnp.float32) * scale

    def gain(shape):
        return 1.0 + normal(shape, 0.01)

    u = jax.random.uniform(next(ks), (DEPTH, 2, LRU_WIDTH), jnp.float32, minval=0.9, maxval=0.999)
    return {
        'x': normal((BATCH, SEQ, D_MODEL), 1.0),
        'c': normal((BATCH, D_MODEL), 1.0),
        'ctx': normal((BATCH, CTX_LEN, D_MODEL), 1.0),
        'c_ctx': normal((D_MODEL,), 1.0),
        'w_mod': normal((DEPTH, D_MODEL, MOD_CHUNKS * D_MODEL), 0.5 * D_MODEL ** -0.5),
        'b_mod': normal((DEPTH, MOD_CHUNKS * D_MODEL), 0.01),
        'norm_ffn1': gain((DEPTH, D_MODEL)),
        'ffn1_w_in': normal((DEPTH, D_MODEL, 2 * D_FF), D_MODEL ** -0.5),
        'ffn1_w_out': normal((DEPTH, D_FF, D_MODEL), D_FF ** -0.5),
        'norm_mix': gain((DEPTH, D_MODEL)),
        'w_in': normal((DEPTH, D_MODEL, IN_WIDTH), D_MODEL ** -0.5),
        'w_out': normal((DEPTH, MIX_WIDTH, D_MODEL), MIX_WIDTH ** -0.5),
        'lru_conv_w': normal((DEPTH, CONV_W, LRU_WIDTH), CONV_W ** -0.5),
        'lru_conv_b': normal((DEPTH, LRU_WIDTH), 0.01),
        'lru_w_a': normal((DEPTH, 2, LRU_BLOCKS, LRU_BLOCK_W, LRU_BLOCK_W), LRU_BLOCK_W ** -0.5),
        'lru_b_a': normal((DEPTH, 2, LRU_WIDTH), 0.01),
        'lru_w_x': normal((DEPTH, 2, LRU_BLOCKS, LRU_BLOCK_W, LRU_BLOCK_W), LRU_BLOCK_W ** -0.5),
        'lru_b_x': normal((DEPTH, 2, LRU_WIDTH), 0.01),
        'lru_lambda': jnp.log(u) - jnp.log1p(-u),
        'mla_q_norm': gain((DEPTH, MLA_Q_RANK)),
        'mla_w_uq': normal((DEPTH, MLA_Q_RANK, MLA_HEADS * MLA_QK_DIM), MLA_Q_RANK ** -0.5),
        'mla_kv_norm': gain((DEPTH, MLA_KV_RANK)),
        'mla_w_ukv': normal((DEPTH, MLA_KV_RANK, MLA_HEADS * (MLA_NOPE + MLA_V)), MLA_KV_RANK ** -0.5),
        'mla_q_gain': gain((DEPTH, MLA_QK_DIM)),
        'mla_k_gain': gain((DEPTH, MLA_QK_DIM)),
        'na_q_gain': gain((DEPTH, NA_HEAD_DIM)),
        'na_k_gain': gain((DEPTH, NA_HEAD_DIM)),
        'na_rpb': normal((DEPTH, NA_HEADS, 2 * NA_WIN_ROWS - 1, 2 * NA_WIN_COLS - 1), 0.5),
        'norm_ffn2': gain((DEPTH, D_MODEL)),
        'ffn2_w_in': normal((DEPTH, D_MODEL, 2 * D_FF), D_MODEL ** -0.5),
        'ffn2_w_out': normal((DEPTH, D_FF, D_MODEL), D_FF ** -0.5),
    }


def reference(x, c, ctx, c_ctx, w_mod, b_mod, norm_ffn1, ffn1_w_in, ffn1_w_out, norm_mix, w_in, w_out,
              lru_conv_w, lru_conv_b, lru_w_a, lru_b_a, lru_w_x, lru_b_x, lru_lambda,
              mla_q_norm, mla_w_uq, mla_kv_norm, mla_w_ukv, mla_q_gain, mla_k_gain,
              na_q_gain, na_k_gain, na_rpb, norm_ffn2, ffn2_w_in, ffn2_w_out):
    T = x.shape[1]
    cos, sin = axial_rope(T, x.dtype)
    h, hc = x, ctx
    for layer in range(DEPTH):
        last = layer == DEPTH - 1
        mod = jax.nn.silu(c) @ w_mod[layer] + b_mod[layer]
        mod_c = jax.nn.silu(c_ctx)[None, :] @ w_mod[layer] + b_mod[layer]
        sh1, sc1, g1, shm, scm, gm, sh2, sc2, g2 = jnp.split(mod, MOD_CHUNKS, axis=-1)
        csh1, csc1, cg1, cshm, cscm, cgm, csh2, csc2, cg2 = jnp.split(mod_c, MOD_CHUNKS, axis=-1)

        h = h + 0.5 * g1[:, None] * swiglu(modulate(rms_norm(h, norm_ffn1[layer]), sh1, sc1),
                                           ffn1_w_in[layer], ffn1_w_out[layer])
        hc = hc + 0.5 * cg1[:, None] * swiglu(modulate(rms_norm(hc, norm_ffn1[layer]), csh1, csc1),
                                              ffn1_w_in[layer], ffn1_w_out[layer])

        yc, yl = hybrid_mixer(
            modulate(rms_norm(hc, norm_mix[layer]), cshm, cscm),
            modulate(rms_norm(h, norm_mix[layer]), shm, scm),
            w_in[layer], w_out[layer], lru_conv_w[layer], lru_conv_b[layer],
            lru_w_a[layer], lru_b_a[layer], lru_w_x[layer], lru_b_x[layer], lru_lambda[layer],
            mla_q_norm[layer], mla_w_uq[layer], mla_kv_norm[layer], mla_w_ukv[layer],
            mla_q_gain[layer], mla_k_gain[layer], na_q_gain[layer], na_k_gain[layer], na_rpb[layer],
            cos, sin, not last)
        h = h + gm[:, None] * yl

        h = h + 0.5 * g2[:, None] * swiglu(modulate(rms_norm(h, norm_ffn2[layer]), sh2, sc2),
                                           ffn2_w_in[layer], ffn2_w_out[layer])
        if not last:
            hc = hc + cgm[:, None] * yc
            hc = hc + 0.5 * cg2[:, None] * swiglu(modulate(rms_norm(hc, norm_ffn2[layer]), csh2, csc2),
                                                  ffn2_w_in[layer], ffn2_w_out[layer])
    return h
```

```python
import functools
import math

import jax
import jax.numpy as jnp
from jax import lax
from jax.experimental import pallas as pl
from jax.experimental.pallas import tpu as pltpu

F32 = jnp.float32
BF16 = jnp.bfloat16

D = 1024
B = 4
SEQ = 8192
DEPTH = 4
GRID_W = 64
GRID_H = SEQ // GRID_W
CTX = 256
TT = CTX + SEQ
D_FF = 2816
EPS = 1e-6

LRU_W = 384
LRU_BW = 64
LRU_C = 8.0
CONV_W = 4
CONV_LEFT = 2

MLA_H = 6
MLA_QR = 256
MLA_KVR = 128
MLA_NOPE = 64
MLA_ROPE = 32
MLA_QK = MLA_NOPE + MLA_ROPE
MLA_V = 64
ROPE_BASE = 10000.0

NA_H = 4
NA_DH = 64
NA_W = NA_H * NA_DH
NA_WR = 8
NA_WC = 16

LANES = 128
HEAD_SLOT = LANES
IN_EXT = 2048
TM = 256
NT = TT // TM
V_ROWS = 80
NEG = -1e30
VMEM_LIMIT = 56 * 1024 * 1024

NA_QROWS = 4
NA_GROUP = NA_QROWS * GRID_W
NA_NGROUPS = SEQ // NA_GROUP
NA_BAND_G = 3
NA_TAB_E = 20

MLA_QSCALE = MLA_QK ** -0.5 * math.log2(math.e)
NA_QSCALE = NA_DH ** -0.5


def _cparams(sem):
    return pltpu.CompilerParams(dimension_semantics=sem, vmem_limit_bytes=VMEM_LIMIT)


def _dot(a, b):
    return jnp.dot(a, b, preferred_element_type=F32)


def _dot_nt(a, b):
    return lax.dot_general(a, b, (((1,), (1,)), ((), ())), preferred_element_type=F32)


def _rms(x, n=None):
    n = x.shape[-1] if n is None else n
    return x * lax.rsqrt(jnp.sum(x * x, axis=-1, keepdims=True) * (1.0 / n) + EPS)


def _mod_kernel(c_ref, w_ref, b_ref, o_ref):
    s = jax.nn.silu(c_ref[...]).astype(BF16)
    o_ref[0] = _dot(s, w_ref[0].astype(BF16)) + b_ref[0]


def _mod_call(cc, w_mod, b_mod):
    tn = 1024
    n = w_mod.shape[-1]
    return pl.pallas_call(
        _mod_kernel,
        grid=(DEPTH, n // tn),
        in_specs=[
            pl.BlockSpec((8, D), lambda l, j: (0, 0)),
            pl.BlockSpec((1, D, tn), lambda l, j: (l, 0, j)),
            pl.BlockSpec((1, 1, tn), lambda l, j: (l, 0, j)),
        ],
        out_specs=pl.BlockSpec((1, 8, tn), lambda l, j: (l, 0, j)),
        out_shape=jax.ShapeDtypeStruct((DEPTH, 8, n), F32),
        compiler_params=_cparams(("arbitrary", "arbitrary")),
        name="mod",
    )(cc, w_mod, b_mod.reshape(DEPTH, 1, n))


def _norm_mod(x, g, mod):
    xn = _rms(x) * g
    return xn * (1.0 + mod[1:2]) + mod[0:1]


def _ffn_kernel(x_ref, mod_ref, g_ref, win_ref, wout_ref, o_ref, *, tf):
    x = x_ref[0]
    mod = mod_ref[0, 0, 0]
    xb = _norm_mod(x, g_ref[...], mod).astype(BF16)
    acc = jnp.zeros((x.shape[0], D), F32)
    for c in range(D_FF // tf):
        gate = _dot(xb, win_ref[:, c * tf:(c + 1) * tf])
        up = _dot(xb, win_ref[:, D_FF + c * tf:D_FF + (c + 1) * tf])
        a = (jax.nn.silu(gate) * up).astype(BF16)
        acc = acc + _dot(a, wout_ref[c * tf:(c + 1) * tf, :])
    o_ref[0] = x + 0.5 * mod[2:3] * acc


def _mod_spec(group):
    return pl.BlockSpec((1, 1, 1, 3, D), lambda b, i: (b, jnp.minimum(i, 1), group, 0, 0))


def _tok_spec(width):
    return pl.BlockSpec((1, TM, width), lambda b, i: (b, i, 0))


def _const_spec(shape):
    return pl.BlockSpec(shape, lambda b, i: (0,) * len(shape))


def _ffn_call(h, mod, g, w_in, w_out, group, name):
    return pl.pallas_call(
        functools.partial(_ffn_kernel, tf=1408),
        grid=(B, NT),
        in_specs=[_tok_spec(D), _mod_spec(group), _const_spec((1, D)),
                  _const_spec((D, 2 * D_FF)), _const_spec((D_FF, D))],
        out_specs=_tok_spec(D),
        out_shape=jax.ShapeDtypeStruct((B, TT, D), F32),
        compiler_params=_cparams(("parallel", "arbitrary")),
        name=name,
    )(h, mod, g, w_in, w_out)


def _rope(x, c, s1, s2):
    return x * c + pltpu.roll(x, LANES - 8, 1) * s1 + pltpu.roll(x, 8, 1) * s2


def _half_rms(x, gain):
    lane = lax.broadcasted_iota(jnp.int32, x.shape, 1)
    lo = lane < NA_DH
    sq = x * x
    s_lo = jnp.sum(jnp.where(lo, sq, 0.0), axis=-1, keepdims=True)
    s_hi = jnp.sum(jnp.where(lo, 0.0, sq), axis=-1, keepdims=True)
    ssq = jnp.where(lo, s_lo, s_hi)
    return x * lax.rsqrt(ssq * (1.0 / NA_DH) + EPS) * gain


def _mixin_kernel(x_ref, mod_ref, g_ref, win_ref, qn_ref, wuq_ref, kvn_ref, wukv_ref,
                  mqg_ref, mkg_ref, nqg_ref, nkg_ref, rc_ref, rs1_ref, rs2_ref,
                  lx_ref, lg_ref, qm_ref, km_ref, vm_ref, nq_ref, nk_ref, nv_ref):
    x = x_ref[0]
    xb = _norm_mod(x, g_ref[...], mod_ref[0, 0, 0]).astype(BF16)
    y = _dot(xb, win_ref[...])
    lx_ref[0] = y[:, 0:LRU_W]
    lg_ref[0] = y[:, LRU_W:2 * LRU_W]
    cq = y[:, 768:1024]
    ckv = y[:, 1024:1152]
    kr_slot = y[:, 1152:1280]
    rc, rs1, rs2 = rc_ref[...], rs1_ref[...], rs2_ref[...]

    q = _dot((_rms(cq) * qn_ref[...]).astype(BF16), wuq_ref[...])
    kv = _dot((_rms(ckv) * kvn_ref[...]).astype(BF16), wukv_ref[...])
    for h in range(MLA_H):
        sl = slice(h * HEAD_SLOT, (h + 1) * HEAD_SLOT)
        qh = _rope(_rms(q[:, sl], MLA_QK) * mqg_ref[...], rc, rs1, rs2)
        qm_ref[0, :, sl] = (qh * MLA_QSCALE).astype(BF16)
        kh = _rope(_rms(kv[:, sl] + kr_slot, MLA_QK) * mkg_ref[...], rc, rs1, rs2)
        km_ref[0, :, sl] = kh.astype(BF16)
    vm_ref[0] = kv[:, MLA_H * HEAD_SLOT:].astype(BF16)

    for t in range(NA_W // LANES):
        sl = slice(t * LANES, (t + 1) * LANES)
        nq = _half_rms(y[:, 1280 + t * LANES:1280 + (t + 1) * LANES], nqg_ref[:, sl])
        nq_ref[0, :, sl] = (nq * NA_QSCALE).astype(BF16)
        nk = _half_rms(y[:, 1536 + t * LANES:1536 + (t + 1) * LANES], nkg_ref[:, sl])
        nk_ref[0, :, sl] = nk.astype(BF16)
    nv_ref[0] = y[:, 1792:2048].astype(BF16)


def _mixin_call(h, mod, p, tabs):
    widths = (LRU_W, LRU_W, MLA_H * HEAD_SLOT, MLA_H * HEAD_SLOT, MLA_H * MLA_V, NA_W, NA_W, NA_W)
    dtypes = (F32, F32, BF16, BF16, BF16, BF16, BF16, BF16)
    rope_spec = pl.BlockSpec((TM, LANES), lambda b, i: (i, 0))
    return pl.pallas_call(
        _mixin_kernel,
        grid=(B, NT),
        in_specs=[_tok_spec(D), _mod_spec(1), _const_spec((1, D)), _const_spec((D, IN_EXT)),
                  _const_spec((1, MLA_QR)), _const_spec((MLA_QR, MLA_H * HEAD_SLOT)),
                  _const_spec((1, MLA_KVR)), _const_spec((MLA_KVR, MLA_H * HEAD_SLOT + MLA_H * MLA_V)),
                  _const_spec((1, HEAD_SLOT)), _const_spec((1, HEAD_SLOT)),
                  _const_spec((1, NA_W)), _const_spec((1, NA_W)),
                  rope_spec, rope_spec, rope_spec],
        out_specs=[_tok_spec(w) for w in widths],
        out_shape=[jax.ShapeDtypeStruct((B, TT, w), dt) for w, dt in zip(widths, dtypes)],
        compiler_params=_cparams(("parallel", "arbitrary")),
        name="mix_in",
    )(h, mod, p["norm_mix"], p["w_in"], p["q_norm"], p["w_uq"], p["kv_norm"], p["w_ukv"],
      p["mq_gain"], p["mk_gain"], p["nq_gain"], p["nk_gain"], *tabs)


def _mixout_kernel(h_ref, mod_ref, ylru_ref, ymla_ref, yna_ref, w1_ref, w2_ref, w3_ref, o_ref):
    y = (_dot(ylru_ref[0].astype(BF16), w1_ref[...]) + _dot(ymla_ref[0], w2_ref[...])
         + _dot(yna_ref[0], w3_ref[...]))
    o_ref[0] = h_ref[0] + mod_ref[0, 0, 0][2:3] * y


def _mixout_call(h, mod, ylru, ymla, yna, p):
    return pl.pallas_call(
        _mixout_kernel,
        grid=(B, NT),
        in_specs=[_tok_spec(D), _mod_spec(1), _tok_spec(LRU_W), _tok_spec(MLA_H * MLA_V), _tok_spec(NA_W),
                  _const_spec((LRU_W, D)), _const_spec((MLA_H * MLA_V, D)), _const_spec((NA_W, D))],
        out_specs=_tok_spec(D),
        out_shape=jax.ShapeDtypeStruct((B, TT, D), F32),
        compiler_params=_cparams(("parallel", "arbitrary")),
        name="mix_out",
    )(h, mod, ylru, ymla, yna, p["w_out_lru"], p["w_out_mla"], p["w_out_na"])


LRU_CHUNK = 256
LRU_PAD = 8
LRU_XROWS = TT + 3 * LRU_PAD
SUB = 8


def _scan_block(a, u, carry, reverse):
    row = lax.broadcasted_iota(jnp.int32, a.shape, 0)
    for s in (1, 2, 4):
        if reverse:
            keep = row < SUB - s
            a_s = jnp.where(keep, pltpu.roll(a, SUB - s, 0), 1.0)
            u_s = jnp.where(keep, pltpu.roll(u, SUB - s, 0), 0.0)
        else:
            keep = row >= s
            a_s = jnp.where(keep, pltpu.roll(a, s, 0), 1.0)
            u_s = jnp.where(keep, pltpu.roll(u, s, 0), 0.0)
        u = a * u_s + u
        a = a * a_s
    h = a * carry + u
    edge = h[0:1] if reverse else h[SUB - 1:SUB]
    return h, jnp.broadcast_to(edge, a.shape)


def _lru_kernel(lx_ref, lg_ref, cw_ref, cb_ref, wg_ref, bg_ref, lam_ref, y_ref, xpad, hf, abuf, ubuf):
    zeros_pad = jnp.zeros((LRU_PAD, LANES), F32)
    xpad[0:LRU_PAD] = zeros_pad
    xpad[LRU_PAD:LRU_PAD + CTX] = lx_ref[0, 0:CTX]
    xpad[LRU_PAD + CTX:2 * LRU_PAD + CTX] = zeros_pad
    xpad[2 * LRU_PAD + CTX:2 * LRU_PAD + TT] = lx_ref[0, CTX:TT]
    xpad[2 * LRU_PAD + TT:3 * LRU_PAD + TT] = zeros_pad

    nblk = LRU_CHUNK // SUB

    def gates(c, d):
        base = pl.multiple_of(jnp.where(c == 0, LRU_PAD, 2 * LRU_PAD + c * LRU_CHUNK), SUB)
        xc = jnp.broadcast_to(cb_ref[...], (LRU_CHUNK, LANES))
        for j in range(CONV_W):
            xc = xc + xpad[pl.ds(base + (j - CONV_LEFT), LRU_CHUNK), :] * cw_ref[j:j + 1, :]
        pre = _dot(xc.astype(BF16), wg_ref[0, :, d * 2 * LANES:(d + 1) * 2 * LANES]) \
            + bg_ref[0, :, d * 2 * LANES:(d + 1) * 2 * LANES]
        r = jax.nn.sigmoid(pre[:, :LANES])
        i = jax.nn.sigmoid(pre[:, LANES:])
        log_a = -LRU_C * r * jax.nn.softplus(-lam_ref[d:d + 1, :])
        abuf[...] = jnp.exp(log_a)
        th = jnp.tanh(log_a)
        ubuf[...] = jnp.sqrt(-2.0 * th / (1.0 - th)) * (i * xc)

    def fwd_chunk(c, carry):
        gates(c, 0)
        row0 = pl.multiple_of(c * LRU_CHUNK, SUB)

        def blk(k, cy):
            r0 = pl.multiple_of(k * SUB, SUB)
            h, cy = _scan_block(abuf[pl.ds(r0, SUB), :], ubuf[pl.ds(r0, SUB), :], cy, False)
            hf[pl.ds(row0 + r0, SUB), :] = h
            return cy

        return lax.fori_loop(0, nblk, blk, carry, unroll=4)

    def bwd_chunk(i, carry):
        c = jnp.where(i == 0, 0, NT - i)
        gates(c, 1)
        row0 = pl.multiple_of(c * LRU_CHUNK, SUB)

        def blk(kk, cy):
            r0 = pl.multiple_of((nblk - 1 - kk) * SUB, SUB)
            h, cy = _scan_block(abuf[pl.ds(r0, SUB), :], ubuf[pl.ds(r0, SUB), :], cy, True)
            rows = pl.ds(row0 + r0, SUB)
            y_ref[0, rows, :] = ((hf[rows, :] + h) * jax.nn.gelu(lg_ref[0, rows, :])).astype(y_ref.dtype)
            return cy

        return lax.fori_loop(0, nblk, blk, carry, unroll=4)

    zero = jnp.zeros((SUB, LANES), F32)
    lax.fori_loop(0, NT, fwd_chunk, zero)
    lax.fori_loop(0, NT, bwd_chunk, zero)


def _lru_call(lx, lg, p):
    nl = LRU_W // LANES
    seq_spec = pl.BlockSpec((1, TT, LANES), lambda b, j: (b, 0, j))
    return pl.pallas_call(
        _lru_kernel,
        grid=(B, nl),
        in_specs=[seq_spec, seq_spec,
                  pl.BlockSpec((CONV_W, LANES), lambda b, j: (0, j)),
                  pl.BlockSpec((1, LANES), lambda b, j: (0, j)),
                  pl.BlockSpec((1, LANES, 4 * LANES), lambda b, j: (j, 0, 0)),
                  pl.BlockSpec((1, 1, 4 * LANES), lambda b, j: (j, 0, 0)),
                  pl.BlockSpec((2, LANES), lambda b, j: (0, j))],
        out_specs=seq_spec,
        out_shape=jax.ShapeDtypeStruct((B, TT, LRU_W), F32),
        scratch_shapes=[pltpu.VMEM((LRU_XROWS, LANES), F32), pltpu.VMEM((TT, LANES), F32),
                        pltpu.VMEM((LRU_CHUNK, LANES), F32), pltpu.VMEM((LRU_CHUNK, LANES), F32)],
        compiler_params=_cparams(("parallel", "arbitrary")),
        name="lru",
    )(lx, lg, p["conv_w"], p["conv_b"], p["lru_wg"], p["lru_bg"], p["lru_lam"])


MLA_TQ = 1024
MLA_TK = 768


def _attend(q, k_ref, vt_ref, kv_chunks, tk):
    tq = q.shape[0]

    def body(i, carry):
        m, acc = carry
        k0 = i * tk if isinstance(i, int) else pl.multiple_of(i * tk, tk)
        s = _dot_nt(k_ref[0, pl.ds(k0, tk), :], q)
        m_new = jnp.maximum(m, jnp.max(s, axis=0, keepdims=True))
        p = jnp.exp2(s - m_new).astype(BF16)
        acc = acc * jnp.exp2(m - m_new) + _dot(vt_ref[0, 0, :, pl.ds(k0, tk)], p)
        return m_new, acc

    init = (jnp.full((1, tq), NEG, F32), jnp.zeros((V_ROWS, tq), F32))
    if kv_chunks == 1:
        _, acc = body(0, init)
    else:
        _, acc = lax.fori_loop(0, kv_chunks, body, init)
    return acc[:MLA_V] / acc[MLA_V:MLA_V + 1]


def _mla_kernel(q_ref, k_ref, vt_ref, o_ref):
    oc = _attend(q_ref[0, 0:CTX, :], k_ref, vt_ref, 1, CTX)
    o_ref[0, 0, 0] = oc.astype(o_ref.dtype)

    def qtile(t, _):
        q0 = pl.multiple_of(CTX + t * MLA_TQ, TM)
        o = _attend(q_ref[0, pl.ds(q0, MLA_TQ), :], k_ref, vt_ref, TT // MLA_TK, MLA_TK)
        for g in range(MLA_TQ // TM):
            o_ref[0, 0, 1 + t * (MLA_TQ // TM) + g] = o[:, g * TM:(g + 1) * TM].astype(o_ref.dtype)
        return 0

    lax.fori_loop(0, SEQ // MLA_TQ, qtile, 0)


def _mla_call(qm, km, vt):
    return pl.pallas_call(
        _mla_kernel,
        grid=(B, MLA_H),
        in_specs=[pl.BlockSpec((1, TT, HEAD_SLOT), lambda b, h: (b, 0, h)),
                  pl.BlockSpec((1, TT, HEAD_SLOT), lambda b, h: (b, 0, h)),
                  pl.BlockSpec((1, 1, V_ROWS, TT), lambda b, h: (b, h, 0, 0))],
        out_specs=pl.BlockSpec((1, 1, NT, MLA_V, TM), lambda b, h: (b, h, 0, 0, 0)),
        out_shape=jax.ShapeDtypeStruct((B, MLA_H, NT, MLA_V, TM), BF16),
        compiler_params=_cparams(("parallel", "arbitrary")),
        name="mla_attn",
    )(qm, km, vt)


def _na_kernel(q_ref, k_ref, vt_ref, tab_ref, o_ref):
    j = pl.program_id(2)
    q = q_ref[0, 0, 0]
    s_ctx = _dot_nt(k_ref[0, 0, 0], q)

    def finish(m, parts):
        acc = jnp.zeros((V_ROWS, TM), F32)
        for vt, s in parts:
            acc = acc + _dot(vt, jnp.exp(s - m).astype(BF16))
        o_ref[0, 0, 0] = (acc[:NA_DH] / acc[NA_DH:NA_DH + 1]).astype(o_ref.dtype)

    @pl.when(j == 0)
    def _():
        finish(jnp.max(s_ctx, axis=0, keepdims=True), [(vt_ref[0, 0, 0], s_ctx)])

    @pl.when(j > 0)
    def _():
        jj = j - 1
        g0 = jnp.clip(jj - 1, 0, NA_NGROUPS - NA_BAND_G)
        lane = lax.broadcasted_iota(jnp.int32, (1, TM), 1)
        qrow = NA_QROWS * jj + jnp.right_shift(lane, 6)
        r0q = jnp.clip(qrow - NA_WR // 2, 0, GRID_H - NA_WR)
        parts = [(vt_ref[0, 0, 0], s_ctx)]
        m = jnp.max(s_ctx, axis=0, keepdims=True)
        for g in range(NA_BAND_G):
            s = _dot_nt(k_ref[0, 0, 1 + g0 + g], q)
            rows = []
            for i in range(NA_QROWS):
                krow = NA_QROWS * (g0 + g) + i
                e = krow - NA_QROWS * jj + NA_WR
                valid = (krow >= r0q) & (krow < r0q + NA_WR)
                rows.append(s[i * GRID_W:(i + 1) * GRID_W] + jnp.where(valid, tab_ref[0, e], NEG))
            s = jnp.concatenate(rows, axis=0)
            m = jnp.maximum(m, jnp.max(s, axis=0, keepdims=True))
            parts.append((vt_ref[0, 0, 1 + g0 + g], s))
        finish(m, parts)


def _na_call(q, k, vt, tab):
    return pl.pallas_call(
        _na_kernel,
        grid=(B, NA_H, NT),
        in_specs=[pl.BlockSpec((1, 1, 1, TM, NA_DH), lambda b, h, j: (b, h, j, 0, 0)),
                  pl.BlockSpec((1, 1, NT, TM, NA_DH), lambda b, h, j: (b, h, 0, 0, 0)),
                  pl.BlockSpec((1, 1, NT, V_ROWS, TM), lambda b, h, j: (b, h, 0, 0, 0)),
                  pl.BlockSpec((1, NA_TAB_E, GRID_W, TM), lambda b, h, j: (h, 0, 0, 0))],
        out_specs=pl.BlockSpec((1, 1, 1, NA_DH, TM), lambda b, h, j: (b, h, j, 0, 0)),
        out_shape=jax.ShapeDtypeStruct((B, NA_H, NT, NA_DH, TM), BF16),
        compiler_params=_cparams(("parallel", "parallel", "arbitrary")),
        name="na_attn",
    )(q, k, vt, tab)


def _rope_tables():
    t = jnp.arange(SEQ)
    pos = jnp.stack([t // GRID_W, t % GRID_W], axis=-1).astype(F32)
    half = MLA_ROPE // 2
    inv = ROPE_BASE ** (-jnp.arange(0, half, 2, dtype=F32) / half)
    ang = pos[:, :, None] * inv
    cos, sin = jnp.cos(ang), jnp.sin(ang)
    ones = jnp.ones((SEQ, MLA_NOPE), F32)
    zeros8 = jnp.zeros((SEQ, 8), F32)
    c = jnp.concatenate([ones, cos[:, 0], cos[:, 0], cos[:, 1], cos[:, 1], ones[:, :32]], axis=-1)
    s1 = jnp.concatenate([0 * ones, -sin[:, 0], zeros8, -sin[:, 1], zeros8, 0 * ones[:, :32]], axis=-1)
    s2 = jnp.concatenate([0 * ones, zeros8, sin[:, 0], zeros8, sin[:, 1], 0 * ones[:, :32]], axis=-1)
    ctx_c = jnp.ones((CTX, LANES), F32)
    ctx_s = jnp.zeros((CTX, LANES), F32)
    return (jnp.concatenate([ctx_c, c]), jnp.concatenate([ctx_s, s1]), jnp.concatenate([ctx_s, s2]))


def _na_tables(rpb):
    e = jnp.arange(NA_TAB_E)[:, None, None, None]
    kc = jnp.arange(GRID_W)[None, :, None, None]
    a = jnp.arange(NA_QROWS)[None, None, :, None]
    qc = jnp.arange(GRID_W)[None, None, None, :]
    row_off = e - 1 - a
    win = jnp.clip(qc - NA_WC // 2, 0, GRID_W - NA_WC)
    ok = (row_off >= 0) & (row_off <= 2 * NA_WR - 2) & (kc >= win) & (kc < win + NA_WC)
    col_off = jnp.clip(kc - qc + NA_WC - 1, 0, 2 * NA_WC - 2)
    ro = jnp.clip(row_off, 0, 2 * NA_WR - 2)
    ro, col_off, ok = jnp.broadcast_arrays(ro, col_off, ok)
    vals = rpb[:, :, ro, col_off]
    tab = jnp.where(ok, vals, NEG)
    return tab.reshape(DEPTH, NA_H, NA_TAB_E, GRID_W, TM)


def _block_diag(w):
    w = w.reshape(DEPTH, 2, LRU_W // LANES, 2, LRU_BW, LRU_BW)
    z = jnp.zeros_like(w[:, :, :, 0])
    top = jnp.concatenate([w[:, :, :, 0], z], axis=-1)
    bot = jnp.concatenate([z, w[:, :, :, 1]], axis=-1)
    return jnp.concatenate([top, bot], axis=-2)


def _prepare(w_in, w_out, lru_conv_b, lru_w_a, lru_b_a, lru_w_x, lru_b_x, mla_w_uq, mla_w_ukv,
             mla_q_gain, mla_k_gain, na_q_gain, na_k_gain):
    z = lambda n: jnp.zeros((DEPTH, D, n), F32)
    kr = w_in[..., 1152:1184]
    w_in_ext = jnp.concatenate(
        [w_in[..., :1152], z(MLA_NOPE), kr, z(HEAD_SLOT - MLA_QK), w_in[..., 1184:]], axis=-1).astype(BF16)
    w_uq = mla_w_uq.reshape(DEPTH, MLA_QR, MLA_H, MLA_QK)
    w_uq = jnp.pad(w_uq, ((0, 0), (0, 0), (0, 0), (0, HEAD_SLOT - MLA_QK))).reshape(DEPTH, MLA_QR, -1)
    w_ukv = mla_w_ukv.reshape(DEPTH, MLA_KVR, MLA_H, MLA_NOPE + MLA_V)
    w_uk = jnp.pad(w_ukv[..., :MLA_NOPE], ((0, 0), (0, 0), (0, 0), (0, HEAD_SLOT - MLA_NOPE)))
    w_ukv_ext = jnp.concatenate([w_uk.reshape(DEPTH, MLA_KVR, -1),
                                 w_ukv[..., MLA_NOPE:].reshape(DEPTH, MLA_KVR, -1)], axis=-1)
    pad_gain = lambda g: jnp.pad(g, ((0, 0), (0, HEAD_SLOT - MLA_QK))).reshape(DEPTH, 1, HEAD_SLOT)
    tile_gain = lambda g: jnp.tile(g, (1, NA_H)).reshape(DEPTH, 1, NA_W)
    wa, wx = _block_diag(lru_w_a), _block_diag(lru_w_x)
    wg = jnp.concatenate([wa[:, 0], wx[:, 0], wa[:, 1], wx[:, 1]], axis=-1).astype(BF16)
    ba = lru_b_a.reshape(DEPTH, 2, LRU_W // LANES, 1, LANES)
    bx = lru_b_x.reshape(DEPTH, 2, LRU_W // LANES, 1, LANES)
    bg = jnp.concatenate([ba[:, 0], bx[:, 0], ba[:, 1], bx[:, 1]], axis=-1)
    w_out = w_out.astype(BF16)
    return dict(
        w_in=w_in_ext, w_uq=w_uq.astype(BF16), w_ukv=w_ukv_ext.astype(BF16),
        mq_gain=pad_gain(mla_q_gain), mk_gain=pad_gain(mla_k_gain),
        nq_gain=tile_gain(na_q_gain), nk_gain=tile_gain(na_k_gain),
        lru_wg=wg, lru_bg=bg, conv_b=lru_conv_b.reshape(DEPTH, 1, LRU_W),
        w_out_lru=w_out[:, :LRU_W], w_out_mla=w_out[:, LRU_W:LRU_W + MLA_H * MLA_V],
        w_out_na=w_out[:, LRU_W + MLA_H * MLA_V:],
    )


def _heads_t(y, nh):
    return y.transpose(0, 2, 4, 1, 3).reshape(B, TT, nh * y.shape[3])


def _v_transposed(v, nh):
    vt = v.reshape(B, TT, nh, MLA_V).transpose(0, 2, 3, 1)
    return jnp.concatenate([vt, jnp.ones((B, nh, V_ROWS - MLA_V, TT), v.dtype)], axis=2)


def kernel(x, c, ctx, c_ctx, w_mod, b_mod, norm_ffn1, ffn1_w_in, ffn1_w_out, norm_mix, w_in, w_out, lru_conv_w, lru_conv_b, lru_w_a, lru_b_a, lru_w_x, lru_b_x, lru_lambda, mla_q_norm, mla_w_uq, mla_kv_norm, mla_w_ukv, mla_q_gain, mla_k_gain, na_q_gain, na_k_gain, na_rpb, norm_ffn2, ffn2_w_in, ffn2_w_out):
    cc = jnp.concatenate([c, c_ctx[None], jnp.zeros((8 - B - 1, D), F32)], axis=0)
    mod = _mod_call(cc, w_mod, b_mod)
    mod = mod.reshape(DEPTH, 8, 3, 3, D)
    mod = jnp.stack([jnp.broadcast_to(mod[:, B:B + 1], (DEPTH, B, 3, 3, D)), mod[:, :B]], axis=2)

    prm = _prepare(w_in, w_out, lru_conv_b, lru_w_a, lru_b_a, lru_w_x, lru_b_x, mla_w_uq, mla_w_ukv,
                   mla_q_gain, mla_k_gain, na_q_gain, na_k_gain)
    prm.update(norm_mix=norm_mix.reshape(DEPTH, 1, D), q_norm=mla_q_norm.reshape(DEPTH, 1, MLA_QR),
               kv_norm=mla_kv_norm.reshape(DEPTH, 1, MLA_KVR), conv_w=lru_conv_w, lru_lam=lru_lambda)
    f1_in, f1_out = ffn1_w_in.astype(BF16), ffn1_w_out.astype(BF16)
    f2_in, f2_out = ffn2_w_in.astype(BF16), ffn2_w_out.astype(BF16)
    g1, g2 = norm_ffn1.reshape(DEPTH, 1, D), norm_ffn2.reshape(DEPTH, 1, D)
    rope_tabs = _rope_tables()
    na_tabs = _na_tables(na_rpb)

    h = jnp.concatenate([ctx, x], axis=1)
    for l in range(DEPTH):
        p = {k: v[l] for k, v in prm.items()}
        h = _ffn_call(h, mod[l], g1[l], f1_in[l], f1_out[l], 0, "ffn1")
        lx, lg, qm, km, vm, nq, nk, nv = _mixin_call(h, mod[l], p, rope_tabs)
        ylru = _lru_call(lx, lg, p)
        ymla = _heads_t(_mla_call(qm, km, _v_transposed(vm, MLA_H)), MLA_H)
        to_groups = lambda a: a.reshape(B, NT, TM, NA_H, NA_DH).transpose(0, 3, 1, 2, 4)
        nvt = nv.reshape(B, NT, TM, NA_H, NA_DH).transpose(0, 3, 1, 4, 2)
        nvt = jnp.concatenate([nvt, jnp.ones((B, NA_H, NT, V_ROWS - NA_DH, TM), BF16)], axis=3)
        yna = _heads_t(_na_call(to_groups(nq), to_groups(nk), nvt, na_tabs[l]), NA_H)
        h = _mixout_call(h, mod[l], ylru, ymla, yna, p)
        h = _ffn_call(h, mod[l], g2[l], f2_in[l], f2_out[l], 2, "ffn2")
    return h[:, CTX:]
```

```python
import functools
import math

import jax
import jax.numpy as jnp
import numpy as np
from jax import lax
from jax.experimental import pallas as pl
from jax.experimental.pallas import tpu as pltpu

F32 = jnp.float32
BF16 = jnp.bfloat16

D = 1024
B = 4
SEQ = 8192
DEPTH = 4
GRID_W = 64
GRID_H = SEQ // GRID_W
CTX = 256
TT = CTX + SEQ
D_FF = 2816
EPS = 1e-6

LRU_W = 384
LRU_BW = 64
LRU_C = 8.0
CONV_W = 4
CONV_LEFT = 2

MLA_H = 6
MLA_QR = 256
MLA_KVR = 128
MLA_NOPE = 64
MLA_ROPE = 32
MLA_QK = MLA_NOPE + MLA_ROPE
MLA_V = 64
ROPE_BASE = 10000.0

NA_H = 4
NA_DH = 64
NA_W = NA_H * NA_DH
NA_WR = 8
NA_WC = 16

LANES = 128
HEAD_SLOT = LANES
IN_EXT = 2048
TM = 256
NT = TT // TM
V_ROWS = 80
NEG = -1e30
VMEM_LIMIT = 56 * 1024 * 1024

NA_QROWS = 4
NA_GROUP = NA_QROWS * GRID_W
NA_NGROUPS = SEQ // NA_GROUP
NA_BAND_G = 3

MLA_QSCALE = MLA_QK ** -0.5 * math.log2(math.e)
NA_QSCALE = NA_DH ** -0.5


def _cparams(sem):
    return pltpu.CompilerParams(dimension_semantics=sem, vmem_limit_bytes=VMEM_LIMIT)


def _dot(a, b):
    return jnp.dot(a, b, preferred_element_type=F32)


def _dot_nt(a, b):
    return lax.dot_general(a, b, (((1,), (1,)), ((), ())), preferred_element_type=F32)


def _rms(x, n=None):
    n = x.shape[-1] if n is None else n
    return x * lax.rsqrt(jnp.sum(x * x, axis=-1, keepdims=True) * (1.0 / n) + EPS)


def _mod_kernel(c_ref, w_ref, b_ref, o_ref):
    s = jax.nn.silu(c_ref[...]).astype(BF16)
    o_ref[0] = _dot(s, w_ref[0].astype(BF16)) + b_ref[0]


def _mod_call(cc, w_mod, b_mod):
    tn = 1024
    n = w_mod.shape[-1]
    return pl.pallas_call(
        _mod_kernel,
        grid=(DEPTH, n // tn),
        in_specs=[
            pl.BlockSpec((8, D), lambda l, j: (0, 0)),
            pl.BlockSpec((1, D, tn), lambda l, j: (l, 0, j)),
            pl.BlockSpec((1, 1, tn), lambda l, j: (l, 0, j)),
        ],
        out_specs=pl.BlockSpec((1, 8, tn), lambda l, j: (l, 0, j)),
        out_shape=jax.ShapeDtypeStruct((DEPTH, 8, n), F32),
        compiler_params=_cparams(("arbitrary", "arbitrary")),
        name="mod",
    )(cc, w_mod, b_mod.reshape(DEPTH, 1, n))


TOK = 768
GROUPS = TOK // TM
NTOK = TT // TOK


def _norm_mod(x, g, mod_ref):
    row = pl.program_id(1) * TOK + lax.broadcasted_iota(jnp.int32, (x.shape[0], 1), 0)
    is_ctx = row < CTX
    mod_c, mod_l = mod_ref[0, 0, 0], mod_ref[0, 1, 0]
    pick = lambda r: jnp.where(is_ctx, mod_c[r:r + 1], mod_l[r:r + 1])
    xn = _rms(x) * g
    return xn * (1.0 + pick(1)) + pick(0), pick(2)


MXU_TILE = 256
FFN_SPLIT = (D_FF // MXU_TILE + 1) // 2 * MXU_TILE
FFN_CHUNKS = ((0, FFN_SPLIT), (FFN_SPLIT, D_FF))


def _ffn_kernel(x_ref, mod_ref, g_ref, win_ref, wout_ref, o_ref):
    x = x_ref[0]
    xm, gate_row = _norm_mod(x, g_ref[...], mod_ref)
    xb = xm.astype(BF16)
    acc = jnp.zeros((x.shape[0], D), F32)
    for lo, hi in FFN_CHUNKS:
        gate = _dot(xb, win_ref[:, lo:hi])
        up = _dot(xb, win_ref[:, D_FF + lo:D_FF + hi])
        a = (jax.nn.silu(gate) * up).astype(BF16)
        acc = acc + _dot(a, wout_ref[lo:hi, :])
    o_ref[0] = x + 0.5 * gate_row * acc


def _mod_spec(group):
    return pl.BlockSpec((1, 2, 1, 3, D), lambda b, i: (b, 0, group, 0, 0))


def _tok_spec(width):
    return pl.BlockSpec((1, TOK, width), lambda b, i: (b, i, 0))


def _const_spec(shape):
    return pl.BlockSpec(shape, lambda b, i: (0,) * len(shape), pipeline_mode=pl.Buffered(1))


def _ffn_call(h, mod, g, w_in, w_out, group, name):
    return pl.pallas_call(
        _ffn_kernel,
        grid=(B, NTOK),
        in_specs=[_tok_spec(D), _mod_spec(group), _const_spec((1, D)),
                  _const_spec((D, 2 * D_FF)), _const_spec((D_FF, D))],
        out_specs=_tok_spec(D),
        out_shape=jax.ShapeDtypeStruct((B, TT, D), F32),
        compiler_params=_cparams(("parallel", "arbitrary")),
        name=name,
    )(h, mod, g, w_in, w_out)


def _group_ssq(x, ones_ref):
    sq = x * x
    hi = sq.astype(BF16)
    lo = (sq - hi.astype(F32)).astype(BF16)
    return _dot(hi, ones_ref[...]) + _dot(lo, ones_ref[...])


def _mixin_kernel(x_ref, mod_ref, g_ref, win_ref, qn_ref, wuq_ref, kvn_ref, wukv_ref,
                  mqg_ref, mkg_ref, nqg_ref, nkg_ref, rc_ref, rs_ref, ones_h_ref, ones_n_ref,
                  lx_ref, lg_ref, qm_ref, km_ref, vm_ref, nq_ref, nk_ref, nv_ref):
    x = x_ref[0]
    xb = _norm_mod(x, g_ref[...], mod_ref)[0].astype(BF16)
    y = _dot(xb, win_ref[...])
    lx_ref[0] = y[:, 0:LRU_W]
    lg_ref[0] = y[:, LRU_W:2 * LRU_W]
    cq = y[:, 768:1024]
    ckv = y[:, 1024:1152]
    kr_slot = y[:, 1152:1280]
    rc, rs = rc_ref[...], rs_ref[...]
    n_slot = MLA_H * HEAD_SLOT

    q2 = _dot((_rms(cq) * qn_ref[...]).astype(BF16), wuq_ref[...])
    q_raw = q2[:, :n_slot]
    inv_q = lax.rsqrt(_group_ssq(q_raw, ones_h_ref) * (1.0 / MLA_QK) + EPS)
    cos_q = rc * (mqg_ref[0:1] * MLA_QSCALE)
    sin_q = rs * (mqg_ref[1:2] * MLA_QSCALE)

    kv = _dot((_rms(ckv) * kvn_ref[...]).astype(BF16), wukv_ref[...])
    lane = lax.broadcasted_iota(jnp.int32, kr_slot.shape, 1)
    kr = jnp.where(lane < MLA_QK, kr_slot, 0.0)
    kr_partner = pltpu.roll(kr_slot, LANES - MLA_ROPE, 1)
    k_raw = kv[:, :n_slot] + jnp.concatenate([kr] * MLA_H, axis=1)
    inv_k = lax.rsqrt(_group_ssq(k_raw, ones_h_ref) * (1.0 / MLA_QK) + EPS)
    cos_k = rc * mkg_ref[0:1]
    sin_k = kr_partner * (rs * mkg_ref[1:2])
    for h in range(MLA_H):
        sl = slice(h * HEAD_SLOT, (h + 1) * HEAD_SLOT)
        qh = q_raw[:, sl] * cos_q + q2[:, n_slot + h * HEAD_SLOT:n_slot + (h + 1) * HEAD_SLOT] * sin_q
        qm_ref[0, :, sl] = (qh * inv_q[:, sl]).astype(BF16)
        km_ref[0, :, sl] = ((k_raw[:, sl] * cos_k + sin_k) * inv_k[:, sl]).astype(BF16)
    for s in range(GROUPS):
        vm_ref[0, s] = kv[s * TM:(s + 1) * TM, n_slot:].T.astype(BF16)

    nq = y[:, 1280:1536]
    nq_ref[0] = (nq * lax.rsqrt(_group_ssq(nq, ones_n_ref) * (1.0 / NA_DH) + EPS)
                 * (nqg_ref[...] * NA_QSCALE)).astype(BF16)
    nk = y[:, 1536:1792]
    nk_ref[0] = (nk * lax.rsqrt(_group_ssq(nk, ones_n_ref) * (1.0 / NA_DH) + EPS) * nkg_ref[...]).astype(BF16)
    for s in range(GROUPS):
        nv_ref[0, s] = y[s * TM:(s + 1) * TM, 1792:2048].T.astype(BF16)


def _tile_t_spec(rows):
    return pl.BlockSpec((1, GROUPS, rows, TM), lambda b, i: (b, i, 0, 0))


def _mixin_call(h, mod, p, tabs):
    widths = (LRU_W, LRU_W, MLA_H * HEAD_SLOT, MLA_H * HEAD_SLOT, NA_W, NA_W)
    dtypes = (F32, F32, BF16, BF16, BF16, BF16)
    rope_spec = pl.BlockSpec((TOK, LANES), lambda b, i: (i, 0))
    tok_shapes = [jax.ShapeDtypeStruct((B, TT, w), dt) for w, dt in zip(widths, dtypes)]
    out_specs = [_tok_spec(w) for w in widths]
    out_specs = out_specs[:4] + [_tile_t_spec(MLA_H * MLA_V)] + out_specs[4:] + [_tile_t_spec(NA_W)]
    out_shape = (tok_shapes[:4] + [jax.ShapeDtypeStruct((B, NT, MLA_H * MLA_V, TM), BF16)] + tok_shapes[4:]
                 + [jax.ShapeDtypeStruct((B, NT, NA_W, TM), BF16)])
    n_slot = MLA_H * HEAD_SLOT
    return pl.pallas_call(
        _mixin_kernel,
        grid=(B, NTOK),
        in_specs=[_tok_spec(D), _mod_spec(1), _const_spec((1, D)), _const_spec((D, IN_EXT)),
                  _const_spec((1, MLA_QR)), _const_spec((MLA_QR, 2 * n_slot)),
                  _const_spec((1, MLA_KVR)), _const_spec((MLA_KVR, n_slot + MLA_H * MLA_V)),
                  _const_spec((2, HEAD_SLOT)), _const_spec((2, HEAD_SLOT)),
                  _const_spec((1, NA_W)), _const_spec((1, NA_W)),
                  rope_spec, rope_spec, _const_spec((n_slot, n_slot)), _const_spec((NA_W, NA_W))],
        out_specs=out_specs,
        out_shape=out_shape,
        compiler_params=_cparams(("parallel", "arbitrary")),
        name="mix_in",
    )(h, mod, p["norm_mix"], p["w_in"], p["q_norm"], p["w_uq"], p["kv_norm"], p["w_ukv"],
      p["mq_gain"], p["mk_gain"], p["nq_gain"], p["nk_gain"], *tabs)


def _dot_tn(a, b):
    return lax.dot_general(a, b, (((0,), (0,)), ((), ())), preferred_element_type=F32)


def _mixout_kernel(h_ref, mod_ref, ylru_ref, ymla_ref, yna_ref, w1_ref, w2_ref, w3_ref, o_ref):
    att = jnp.concatenate([_dot_tn(ymla_ref[0, s], w2_ref[...]) + _dot_tn(yna_ref[0, s], w3_ref[...])
                           for s in range(GROUPS)], axis=0)
    y = _dot(ylru_ref[0].astype(BF16), w1_ref[...]) + att
    row = pl.program_id(1) * TOK + lax.broadcasted_iota(jnp.int32, (TOK, 1), 0)
    gate = jnp.where(row < CTX, mod_ref[0, 0, 0][2:3], mod_ref[0, 1, 0][2:3])
    o_ref[0] = h_ref[0] + gate * y


def _mixout_call(h, mod, ylru, ymla, yna, p):
    return pl.pallas_call(
        _mixout_kernel,
        grid=(B, NTOK),
        in_specs=[_tok_spec(D), _mod_spec(1), _tok_spec(LRU_W), _tile_t_spec(MLA_H * MLA_V), _tile_t_spec(NA_W),
                  _const_spec((LRU_W, D)), _const_spec((MLA_H * MLA_V, D)), _const_spec((NA_W, D))],
        out_specs=_tok_spec(D),
        out_shape=jax.ShapeDtypeStruct((B, TT, D), F32),
        compiler_params=_cparams(("parallel", "arbitrary")),
        name="mix_out",
    )(h, mod, ylru, ymla, yna, p["w_out_lru"], p["w_out_mla"], p["w_out_na"])


LRU_CHUNK = 256
LRU_PAD = 8
LRU_XROWS = TT + 3 * LRU_PAD
SUB = 8


def _scan_block(a, u, carry, reverse):
    row = lax.broadcasted_iota(jnp.int32, a.shape, 0)
    for s in (1, 2, 4):
        if reverse:
            keep = row < SUB - s
            a_s = jnp.where(keep, pltpu.roll(a, SUB - s, 0), 1.0)
            u_s = jnp.where(keep, pltpu.roll(u, SUB - s, 0), 0.0)
        else:
            keep = row >= s
            a_s = jnp.where(keep, pltpu.roll(a, s, 0), 1.0)
            u_s = jnp.where(keep, pltpu.roll(u, s, 0), 0.0)
        u = a * u_s + u
        a = a * a_s
    h = a * carry + u
    edge = h[0:1] if reverse else h[SUB - 1:SUB]
    return h, jnp.broadcast_to(edge, a.shape)


def _lru_kernel(lx_ref, lg_ref, cw_ref, cb_ref, wg_ref, bg_ref, lam_ref, y_ref, xpad, hf, abuf, ubuf):
    zeros_pad = jnp.zeros((LRU_PAD, LANES), F32)
    xpad[0:LRU_PAD] = zeros_pad
    xpad[LRU_PAD:LRU_PAD + CTX] = lx_ref[0, 0:CTX]
    xpad[LRU_PAD + CTX:2 * LRU_PAD + CTX] = zeros_pad
    xpad[2 * LRU_PAD + CTX:2 * LRU_PAD + TT] = lx_ref[0, CTX:TT]
    xpad[2 * LRU_PAD + TT:3 * LRU_PAD + TT] = zeros_pad

    nblk = LRU_CHUNK // SUB

    def gates(c, d):
        base = pl.multiple_of(jnp.where(c == 0, LRU_PAD, 2 * LRU_PAD + c * LRU_CHUNK), SUB)
        xc = jnp.broadcast_to(cb_ref[...], (LRU_CHUNK, LANES))
        for j in range(CONV_W):
            xc = xc + xpad[pl.ds(base + (j - CONV_LEFT), LRU_CHUNK), :] * cw_ref[j:j + 1, :]
        pre = _dot(xc.astype(BF16), wg_ref[0, :, d * 2 * LANES:(d + 1) * 2 * LANES]) \
            + bg_ref[0, :, d * 2 * LANES:(d + 1) * 2 * LANES]
        r = jax.nn.sigmoid(pre[:, :LANES])
        i = jax.nn.sigmoid(pre[:, LANES:])
        log_a = -LRU_C * r * jax.nn.softplus(-lam_ref[d:d + 1, :])
        abuf[...] = jnp.exp(log_a)
        th = jnp.tanh(log_a)
        ubuf[...] = jnp.sqrt(-2.0 * th / (1.0 - th)) * (i * xc)

    def fwd_chunk(c, carry):
        gates(c, 0)
        row0 = pl.multiple_of(c * LRU_CHUNK, SUB)

        def blk(k, cy):
            r0 = pl.multiple_of(k * SUB, SUB)
            h, cy = _scan_block(abuf[pl.ds(r0, SUB), :], ubuf[pl.ds(r0, SUB), :], cy, False)
            hf[pl.ds(row0 + r0, SUB), :] = h
            return cy

        return lax.fori_loop(0, nblk, blk, carry, unroll=4)

    def bwd_chunk(i, carry):
        c = jnp.where(i == 0, 0, NT - i)
        gates(c, 1)
        row0 = pl.multiple_of(c * LRU_CHUNK, SUB)

        def blk(kk, cy):
            r0 = pl.multiple_of((nblk - 1 - kk) * SUB, SUB)
            h, cy = _scan_block(abuf[pl.ds(r0, SUB), :], ubuf[pl.ds(r0, SUB), :], cy, True)
            rows = pl.ds(row0 + r0, SUB)
            y_ref[0, rows, :] = ((hf[rows, :] + h) * jax.nn.gelu(lg_ref[0, rows, :])).astype(y_ref.dtype)
            return cy

        return lax.fori_loop(0, nblk, blk, carry, unroll=4)

    zero = jnp.zeros((SUB, LANES), F32)
    lax.fori_loop(0, NT, fwd_chunk, zero)
    lax.fori_loop(0, NT, bwd_chunk, zero)


def _lru_call(lx, lg, p):
    nl = LRU_W // LANES
    seq_spec = pl.BlockSpec((1, TT, LANES), lambda b, j: (b, 0, j))
    return pl.pallas_call(
        _lru_kernel,
        grid=(B, nl),
        in_specs=[seq_spec, seq_spec,
                  pl.BlockSpec((CONV_W, LANES), lambda b, j: (0, j)),
                  pl.BlockSpec((1, LANES), lambda b, j: (0, j)),
                  pl.BlockSpec((1, LANES, 4 * LANES), lambda b, j: (j, 0, 0)),
                  pl.BlockSpec((1, 1, 4 * LANES), lambda b, j: (j, 0, 0)),
                  pl.BlockSpec((2, LANES), lambda b, j: (0, j))],
        out_specs=seq_spec,
        out_shape=jax.ShapeDtypeStruct((B, TT, LRU_W), F32),
        scratch_shapes=[pltpu.VMEM((LRU_XROWS, LANES), F32), pltpu.VMEM((TT, LANES), F32),
                        pltpu.VMEM((LRU_CHUNK, LANES), F32), pltpu.VMEM((LRU_CHUNK, LANES), F32)],
        compiler_params=_cparams(("parallel", "arbitrary")),
        name="lru",
    )(lx, lg, p["conv_w"], p["conv_b"], p["lru_wg"], p["lru_bg"], p["lru_lam"])


MLA_TQ = 1024
MLA_TK = 256
MLA_NKV = TT // MLA_TK
MLA_UNROLL = 4
MLA_QCOL = 256


def _with_ones(vt):
    return jnp.concatenate([vt, jnp.ones((V_ROWS - vt.shape[0], vt.shape[1]), vt.dtype)], axis=0)


def _softmax_pv(s, vt, m, acc):
    m_new = jnp.maximum(m, jnp.max(s, axis=0, keepdims=True))
    p = jnp.exp2(s - m_new).astype(BF16)
    return m_new, acc * jnp.exp2(m - m_new) + _dot(vt, p)


def _attend(q, k_ref, vt_ref, s_a, s_b):
    tk = MLA_TK
    assert MLA_NKV % MLA_UNROLL == 1 and MLA_UNROLL % 2 == 0
    ncol = MLA_TQ // MLA_QCOL
    cols = [slice(c * MLA_QCOL, (c + 1) * MLA_QCOL) for c in range(ncol)]
    q_cols = [q[sl] for sl in cols]

    def keys(i):
        return k_ref[0, pl.ds(pl.multiple_of(i * tk, tk), tk), :]

    def values(i):
        return _with_ones(vt_ref[0, i])

    k0 = keys(0)
    for c in range(ncol):
        s_a[:, cols[c]] = _dot_nt(k0, q_cols[c])

    def group(j, carry):
        ms, accs = list(carry[0]), list(carry[1])
        cur, nxt = s_a, s_b
        for u in range(MLA_UNROLL):
            i = MLA_UNROLL * j + u
            k_next, v_cur = keys(i + 1), values(i)
            for c in range(ncol):
                nxt[:, cols[c]] = _dot_nt(k_next, q_cols[c])
                ms[c], accs[c] = _softmax_pv(cur[:, cols[c]], v_cur, ms[c], accs[c])
            cur, nxt = nxt, cur
        return tuple(ms), tuple(accs)

    init = (tuple(jnp.full((1, MLA_QCOL), NEG, F32) for _ in cols),
            tuple(jnp.zeros((V_ROWS, MLA_QCOL), F32) for _ in cols))
    ms, accs = lax.fori_loop(0, MLA_NKV // MLA_UNROLL, group, init)
    v_last = values(MLA_NKV - 1)
    outs = []
    for c in range(ncol):
        _, acc = _softmax_pv(s_a[:, cols[c]], v_last, ms[c], accs[c])
        outs.append(acc[:MLA_V] / acc[MLA_V:MLA_V + 1])
    return jnp.concatenate(outs, axis=1)


def _mla_kernel(q_ref, k_ref, vt_ref, o_ref, s_a, s_b):
    qc = q_ref[0, 0:CTX, :]
    sc = _dot_nt(k_ref[0, 0:CTX, :], qc)
    _, acc = _softmax_pv(sc, _with_ones(vt_ref[0, 0]), jnp.full((1, CTX), NEG, F32),
                         jnp.zeros((V_ROWS, CTX), F32))
    o_ref[0, 0] = (acc[:MLA_V] / acc[MLA_V:MLA_V + 1]).astype(o_ref.dtype)

    def qtile(t, _):
        q0 = pl.multiple_of(CTX + t * MLA_TQ, TM)
        o = _attend(q_ref[0, pl.ds(q0, MLA_TQ), :], k_ref, vt_ref, s_a, s_b)
        for g in range(MLA_TQ // TM):
            o_ref[0, 1 + t * (MLA_TQ // TM) + g] = o[:, g * TM:(g + 1) * TM].astype(o_ref.dtype)
        return 0

    lax.fori_loop(0, SEQ // MLA_TQ, qtile, 0)


def _mla_call(qm, km, vt):
    assert MLA_TK == TM
    head_t_spec = pl.BlockSpec((1, NT, MLA_V, TM), lambda b, h: (b, 0, h, 0))
    return pl.pallas_call(
        _mla_kernel,
        grid=(B, MLA_H),
        in_specs=[pl.BlockSpec((1, TT, HEAD_SLOT), lambda b, h: (b, 0, h)),
                  pl.BlockSpec((1, TT, HEAD_SLOT), lambda b, h: (b, 0, h)),
                  head_t_spec],
        out_specs=head_t_spec,
        out_shape=jax.ShapeDtypeStruct((B, NT, MLA_H * MLA_V, TM), BF16),
        scratch_shapes=[pltpu.VMEM((MLA_TK, MLA_TQ), F32), pltpu.VMEM((MLA_TK, MLA_TQ), F32)],
        compiler_params=_cparams(("parallel", "arbitrary")),
        name="mla_attn",
    )(qm, km, vt)


def _na_kernel(q_ref, k_ref, vt_ref, tab_ref, o_ref):
    j = pl.program_id(1)
    q_all = q_ref[0]
    lane_head = jnp.right_shift(lax.broadcasted_iota(jnp.int32, q_all.shape, 1), 6)
    q_heads = [jnp.where(lane_head == h, q_all, jnp.zeros_like(q_all)) for h in range(NA_H)]

    def values(g, h):
        return vt_ref[0, g, h * NA_DH:(h + 1) * NA_DH, :]

    def finish(h, s, vt):
        m = jnp.max(s, axis=0, keepdims=True)
        acc = _dot(_with_ones(vt), jnp.exp(s - m).astype(BF16))
        o_ref[0, 0, h * NA_DH:(h + 1) * NA_DH, :] = (acc[:NA_DH] / acc[NA_DH:NA_DH + 1]).astype(o_ref.dtype)

    @pl.when(j == 0)
    def _():
        for h in range(NA_H):
            finish(h, _dot_nt(k_ref[0, 0], q_heads[h]), values(0, h))

    @pl.when(j > 0)
    def _():
        jj = j - 1
        g0 = jnp.clip(jj - 1, 0, NA_NGROUPS - NA_BAND_G)
        variant = jnp.where(jj == 0, 0, jnp.where(jj == NA_NGROUPS - 1, 2, 1))
        band = k_ref[0, pl.ds(1 + g0, NA_BAND_G)].reshape(NA_BAND_G * TM, NA_W)
        keys = jnp.concatenate([k_ref[0, 0], band], axis=0)
        scores, probs = {}, {}
        for t in range(NA_H + 2):
            if t < NA_H:
                scores[t] = _dot_nt(keys, q_heads[t])
            h = t - 1
            if 0 <= h < NA_H:
                s = scores.pop(h)
                s = jnp.concatenate([s[:TM], s[TM:] + tab_ref[variant, h]], axis=0)
                probs[h] = jnp.exp(s - jnp.max(s, axis=0, keepdims=True)).astype(BF16)
            h = t - 2
            if 0 <= h < NA_H:
                vt = jnp.concatenate([values(0, h)] + [values(1 + g0 + g, h) for g in range(NA_BAND_G)], axis=1)
                acc = _dot(_with_ones(vt), probs.pop(h))
                o_ref[0, 0, h * NA_DH:(h + 1) * NA_DH, :] = (
                    acc[:NA_DH] / acc[NA_DH:NA_DH + 1]).astype(o_ref.dtype)


def _na_call(q, k, vt, tab):
    whole = pl.BlockSpec((1, NT, TM, NA_W), lambda b, j: (b, 0, 0, 0))
    return pl.pallas_call(
        _na_kernel,
        grid=(B, NT),
        in_specs=[pl.BlockSpec((1, TM, NA_W), lambda b, j: (b, j, 0)), whole, whole,
                  pl.BlockSpec((3, NA_H, NA_BAND_G * TM, TM), lambda b, j: (0, 0, 0, 0),
                               pipeline_mode=pl.Buffered(1))],
        out_specs=pl.BlockSpec((1, 1, NA_W, TM), lambda b, j: (b, j, 0, 0)),
        out_shape=jax.ShapeDtypeStruct((B, NT, NA_W, TM), BF16),
        compiler_params=_cparams(("parallel", "arbitrary")),
        name="na_attn",
    )(q, k, vt, tab)


def _rope_tables():
    t = jnp.arange(SEQ)
    pos = jnp.stack([t // GRID_W, t % GRID_W], axis=-1).astype(F32)
    half = MLA_ROPE // 2
    inv = ROPE_BASE ** (-jnp.arange(0, half, 2, dtype=F32) / half)
    ang = pos[:, :, None] * inv
    cos, sin = jnp.cos(ang), jnp.sin(ang)
    ones = jnp.ones((SEQ, MLA_NOPE), F32)
    c = jnp.concatenate([ones, cos[:, 0], cos[:, 0], cos[:, 1], cos[:, 1], ones[:, :32]], axis=-1)
    s = jnp.concatenate([0 * ones, -sin[:, 0], sin[:, 0], -sin[:, 1], sin[:, 1], 0 * ones[:, :32]], axis=-1)
    ctx_c = jnp.ones((CTX, LANES), F32)
    ctx_s = jnp.zeros((CTX, LANES), F32)
    return jnp.concatenate([ctx_c, c]), jnp.concatenate([ctx_s, s])


ROPE_PARTNER = np.concatenate([np.arange(8, 16), np.arange(0, 8), np.arange(24, 32), np.arange(16, 24)])


def _group_ones(width, group):
    g = np.arange(width) // group
    return jnp.asarray(g[:, None] == g[None, :], BF16)


def _na_window_pattern():
    n_band = NA_BAND_G * NA_QROWS
    i = np.arange(n_band)[:, None]
    a = np.arange(NA_QROWS)[None, :]

    def pattern(jj):
        g0 = min(max(jj - 1, 0), NA_NGROUPS - NA_BAND_G)
        krow = NA_QROWS * g0 + i
        qrow = NA_QROWS * jj + a
        r0q = np.clip(qrow - NA_WR // 2, 0, GRID_H - NA_WR)
        return (krow >= r0q) & (krow < r0q + NA_WR), krow - qrow + NA_WR - 1

    interior = [pattern(jj) for jj in range(1, NA_NGROUPS - 1)]
    assert all((v == interior[0][0]).all() and (r[v] == interior[0][1][v]).all() for v, r in interior)
    pats = [pattern(0), interior[0], pattern(NA_NGROUPS - 1)]
    valid = np.stack([p[0] for p in pats])
    row_off = np.stack([p[1] for p in pats])
    assert (row_off[valid] >= 0).all() and (row_off[valid] <= 2 * NA_WR - 2).all()
    return valid, row_off


def _na_tables(rpb):
    kc = jnp.arange(GRID_W)[:, None]
    qc = jnp.arange(GRID_W)[None, :]
    win = jnp.clip(qc - NA_WC // 2, 0, GRID_W - NA_WC)
    ok_col = (kc >= win) & (kc < win + NA_WC)
    col_off = kc - qc + NA_WC - 1
    pick_col = ((col_off[None] == jnp.arange(2 * NA_WC - 1)[:, None, None]) & ok_col[None]).astype(F32)
    valid, row_off = _na_window_pattern()
    pick_row = ((row_off[..., None] == jnp.arange(2 * NA_WR - 1)) & valid[..., None]).astype(F32)
    tab = jnp.einsum("vian,dhnj,jkq->dvhikaq", pick_row, rpb, pick_col, precision=lax.Precision.HIGHEST)
    keep = valid[:, None, :, None, :, None] & ok_col[None, None, None, :, None, :]
    tab = jnp.where(keep, tab, NEG)
    return tab.reshape(DEPTH, 3, NA_H, NA_BAND_G * TM, TM)


def _block_diag(w):
    w = w.reshape(DEPTH, 2, LRU_W // LANES, 2, LRU_BW, LRU_BW)
    z = jnp.zeros_like(w[:, :, :, 0])
    top = jnp.concatenate([w[:, :, :, 0], z], axis=-1)
    bot = jnp.concatenate([z, w[:, :, :, 1]], axis=-1)
    return jnp.concatenate([top, bot], axis=-2)


def _prepare(w_in, w_out, lru_conv_b, lru_w_a, lru_b_a, lru_w_x, lru_b_x, mla_w_uq, mla_w_ukv,
             mla_q_gain, mla_k_gain, na_q_gain, na_k_gain):
    z = lambda n: jnp.zeros((DEPTH, D, n), F32)
    kr = w_in[..., 1152:1184]
    w_in_ext = jnp.concatenate(
        [w_in[..., :1152], z(MLA_NOPE), kr, kr[..., ROPE_PARTNER], w_in[..., 1184:]], axis=-1).astype(BF16)
    w_uq = mla_w_uq.reshape(DEPTH, MLA_QR, MLA_H, MLA_QK)
    slot_pad = ((0, 0), (0, 0), (0, 0), (0, HEAD_SLOT - MLA_QK))
    zq = jnp.zeros((DEPTH, MLA_QR, MLA_H, MLA_NOPE), F32)
    w_uq_partner = jnp.concatenate([zq, w_uq[..., MLA_NOPE + ROPE_PARTNER]], axis=-1)
    w_uq = jnp.concatenate([jnp.pad(w_uq, slot_pad).reshape(DEPTH, MLA_QR, -1),
                            jnp.pad(w_uq_partner, slot_pad).reshape(DEPTH, MLA_QR, -1)], axis=-1)
    w_ukv = mla_w_ukv.reshape(DEPTH, MLA_KVR, MLA_H, MLA_NOPE + MLA_V)
    w_uk = jnp.pad(w_ukv[..., :MLA_NOPE], ((0, 0), (0, 0), (0, 0), (0, HEAD_SLOT - MLA_NOPE)))
    w_ukv_ext = jnp.concatenate([w_uk.reshape(DEPTH, MLA_KVR, -1),
                                 w_ukv[..., MLA_NOPE:].reshape(DEPTH, MLA_KVR, -1)], axis=-1)

    def pad_gain(g):
        partner = jnp.concatenate([jnp.zeros((DEPTH, MLA_NOPE), F32), g[:, MLA_NOPE + ROPE_PARTNER]], axis=-1)
        both = jnp.stack([g, partner], axis=1)
        return jnp.pad(both, ((0, 0), (0, 0), (0, HEAD_SLOT - MLA_QK)))
    tile_gain = lambda g: jnp.tile(g, (1, NA_H)).reshape(DEPTH, 1, NA_W)
    wa, wx = _block_diag(lru_w_a), _block_diag(lru_w_x)
    wg = jnp.concatenate([wa[:, 0], wx[:, 0], wa[:, 1], wx[:, 1]], axis=-1).astype(BF16)
    ba = lru_b_a.reshape(DEPTH, 2, LRU_W // LANES, 1, LANES)
    bx = lru_b_x.reshape(DEPTH, 2, LRU_W // LANES, 1, LANES)
    bg = jnp.concatenate([ba[:, 0], bx[:, 0], ba[:, 1], bx[:, 1]], axis=-1)
    w_out = w_out.astype(BF16)
    return dict(
        w_in=w_in_ext, w_uq=w_uq.astype(BF16), w_ukv=w_ukv_ext.astype(BF16),
        mq_gain=pad_gain(mla_q_gain), mk_gain=pad_gain(mla_k_gain),
        nq_gain=tile_gain(na_q_gain), nk_gain=tile_gain(na_k_gain),
        lru_wg=wg, lru_bg=bg, conv_b=lru_conv_b.reshape(DEPTH, 1, LRU_W),
        w_out_lru=w_out[:, :LRU_W], w_out_mla=w_out[:, LRU_W:LRU_W + MLA_H * MLA_V],
        w_out_na=w_out[:, LRU_W + MLA_H * MLA_V:],
    )


def kernel(x, c, ctx, c_ctx, w_mod, b_mod, norm_ffn1, ffn1_w_in, ffn1_w_out, norm_mix, w_in, w_out, lru_conv_w, lru_conv_b, lru_w_a, lru_b_a, lru_w_x, lru_b_x, lru_lambda, mla_q_norm, mla_w_uq, mla_kv_norm, mla_w_ukv, mla_q_gain, mla_k_gain, na_q_gain, na_k_gain, na_rpb, norm_ffn2, ffn2_w_in, ffn2_w_out):
    cc = jnp.concatenate([c, c_ctx[None], jnp.zeros((8 - B - 1, D), F32)], axis=0)
    mod = _mod_call(cc, w_mod, b_mod)
    mod = mod.reshape(DEPTH, 8, 3, 3, D)
    mod = jnp.stack([jnp.broadcast_to(mod[:, B:B + 1], (DEPTH, B, 3, 3, D)), mod[:, :B]], axis=2)

    prm = _prepare(w_in, w_out, lru_conv_b, lru_w_a, lru_b_a, lru_w_x, lru_b_x, mla_w_uq, mla_w_ukv,
                   mla_q_gain, mla_k_gain, na_q_gain, na_k_gain)
    prm.update(norm_mix=norm_mix.reshape(DEPTH, 1, D), q_norm=mla_q_norm.reshape(DEPTH, 1, MLA_QR),
               kv_norm=mla_kv_norm.reshape(DEPTH, 1, MLA_KVR), conv_w=lru_conv_w, lru_lam=lru_lambda)
    f1_in, f1_out = ffn1_w_in.astype(BF16), ffn1_w_out.astype(BF16)
    f2_in, f2_out = ffn2_w_in.astype(BF16), ffn2_w_out.astype(BF16)
    g1, g2 = norm_ffn1.reshape(DEPTH, 1, D), norm_ffn2.reshape(DEPTH, 1, D)
    rope_tabs = _rope_tables() + (_group_ones(MLA_H * HEAD_SLOT, HEAD_SLOT), _group_ones(NA_W, NA_DH))
    na_tabs = _na_tables(na_rpb)

    h = jnp.concatenate([ctx, x], axis=1)
    for l in range(DEPTH):
        p = {k: v[l] for k, v in prm.items()}
        h = _ffn_call(h, mod[l], g1[l], f1_in[l], f1_out[l], 0, "ffn1")
        lx, lg, qm, km, vm, nq, nk, nv = _mixin_call(h, mod[l], p, rope_tabs)
        ylru = _lru_call(lx, lg, p)
        ymla = _mla_call(qm, km, vm)
        yna = _na_call(nq, nk.reshape(B, NT, TM, NA_W), nv, na_tabs[l])
        h = _mixout_call(h, mod[l], ylru, ymla, yna, p)
        h = _ffn_call(h, mod[l], g2[l], f2_in[l], f2_out[l], 2, "ffn2")
    return h[:, CTX:]
```

```python
import functools
import math

import jax
import jax.numpy as jnp
import numpy as np
from jax import lax
from jax.experimental import pallas as pl
from jax.experimental.pallas import tpu as pltpu

F32 = jnp.float32
BF16 = jnp.bfloat16

D = 1024
B = 4
SEQ = 8192
DEPTH = 4
GRID_W = 64
GRID_H = SEQ // GRID_W
CTX = 256
TT = CTX + SEQ
D_FF = 2816
EPS = 1e-6

LRU_W = 384
LRU_BW = 64
LRU_C = 8.0
CONV_W = 4
CONV_LEFT = 2

MLA_H = 6
MLA_QR = 256
MLA_KVR = 128
MLA_NOPE = 64
MLA_ROPE = 32
MLA_QK = MLA_NOPE + MLA_ROPE
MLA_V = 64
ROPE_BASE = 10000.0

NA_H = 4
NA_DH = 64
NA_W = NA_H * NA_DH
NA_WR = 8
NA_WC = 16

LANES = 128
HEAD_SLOT = LANES
IN_EXT = 2048
TM = 256
NT = TT // TM
V_ROWS = 80
NEG = -1e30
VMEM_LIMIT = 56 * 1024 * 1024

NA_QROWS = 4
NA_GROUP = NA_QROWS * GRID_W
NA_NGROUPS = SEQ // NA_GROUP
NA_BAND_G = 3

MLA_QSCALE = MLA_QK ** -0.5 * math.log2(math.e)
NA_QSCALE = NA_DH ** -0.5


def _cparams(sem):
    return pltpu.CompilerParams(dimension_semantics=sem, vmem_limit_bytes=VMEM_LIMIT)


def _dot(a, b):
    return jnp.dot(a, b, preferred_element_type=F32)


def _dot_nt(a, b):
    return lax.dot_general(a, b, (((1,), (1,)), ((), ())), preferred_element_type=F32)


def _rms(x, n=None):
    n = x.shape[-1] if n is None else n
    return x * lax.rsqrt(jnp.sum(x * x, axis=-1, keepdims=True) * (1.0 / n) + EPS)


def _mod_kernel(c_ref, w_ref, b_ref, o_ref):
    s = jax.nn.silu(c_ref[...]).astype(BF16)
    o_ref[0] = _dot(s, w_ref[0].astype(BF16)) + b_ref[0]


def _mod_call(cc, w_mod, b_mod):
    tn = 1024
    n = w_mod.shape[-1]
    return pl.pallas_call(
        _mod_kernel,
        grid=(DEPTH, n // tn),
        in_specs=[
            pl.BlockSpec((8, D), lambda l, j: (0, 0)),
            pl.BlockSpec((1, D, tn), lambda l, j: (l, 0, j)),
            pl.BlockSpec((1, 1, tn), lambda l, j: (l, 0, j)),
        ],
        out_specs=pl.BlockSpec((1, 8, tn), lambda l, j: (l, 0, j)),
        out_shape=jax.ShapeDtypeStruct((DEPTH, 8, n), F32),
        compiler_params=_cparams(("arbitrary", "arbitrary")),
        name="mod",
    )(cc, w_mod, b_mod.reshape(DEPTH, 1, n))


TOK = 768
GROUPS = TOK // TM
NTOK = TT // TOK


def _norm_mod(x, g, mod_ref):
    row = pl.program_id(1) * TOK + lax.broadcasted_iota(jnp.int32, (x.shape[0], 1), 0)
    is_ctx = row < CTX
    mod_c, mod_l = mod_ref[0, 0, 0], mod_ref[0, 1, 0]
    pick = lambda r: jnp.where(is_ctx, mod_c[r:r + 1], mod_l[r:r + 1])
    xn = _rms(x) * g
    return xn * (1.0 + pick(1)) + pick(0), pick(2)


MXU_TILE = 256
FFN_SPLIT = (D_FF // MXU_TILE + 1) // 2 * MXU_TILE
FFN_CHUNKS = ((0, FFN_SPLIT), (FFN_SPLIT, D_FF))


def _ffn_kernel(x_ref, mod_ref, g_ref, win_ref, wout_ref, o_ref):
    x = x_ref[0]
    xm, gate_row = _norm_mod(x, g_ref[...], mod_ref)
    xb = xm.astype(BF16)
    acc = jnp.zeros((x.shape[0], D), F32)
    for lo, hi in FFN_CHUNKS:
        gate = _dot(xb, win_ref[:, lo:hi])
        up = _dot(xb, win_ref[:, D_FF + lo:D_FF + hi])
        a = (jax.nn.silu(gate) * up).astype(BF16)
        acc = acc + _dot(a, wout_ref[lo:hi, :])
    o_ref[0] = x + 0.5 * gate_row * acc


def _mod_spec(group):
    return pl.BlockSpec((1, 2, 1, 3, D), lambda b, i: (b, 0, group, 0, 0))


def _tok_spec(width):
    return pl.BlockSpec((1, TOK, width), lambda b, i: (b, i, 0))


def _const_spec(shape):
    return pl.BlockSpec(shape, lambda b, i: (0,) * len(shape), pipeline_mode=pl.Buffered(1))


def _layer_spec(shape, layer):
    return pl.BlockSpec((None,) + shape, lambda b, i: (layer,) + (0,) * len(shape),
                        pipeline_mode=pl.Buffered(1))


def _ffn_call(h, mod, g, w_in, w_out, layer, group, name):
    return pl.pallas_call(
        _ffn_kernel,
        grid=(B, NTOK),
        in_specs=[_tok_spec(D), _mod_spec(group), _const_spec((1, D)),
                  _layer_spec((D, 2 * D_FF), layer), _layer_spec((D_FF, D), layer)],
        out_specs=_tok_spec(D),
        out_shape=jax.ShapeDtypeStruct((B, TT, D), F32),
        compiler_params=_cparams(("parallel", "arbitrary")),
        name=name,
    )(h, mod, g, w_in, w_out)


def _group_ssq(x, ones_ref):
    sq = x * x
    hi = sq.astype(BF16)
    lo = (sq - hi.astype(F32)).astype(BF16)
    return _dot(hi, ones_ref[...]) + _dot(lo, ones_ref[...])


def _mixin_kernel(x_ref, mod_ref, g_ref, win_ref, qn_ref, wuq_ref, kvn_ref, wukv_ref,
                  mqg_ref, mkg_ref, nqg_ref, nkg_ref, rc_ref, rs_ref, ones_h_ref, ones_n_ref,
                  lx_ref, lg_ref, qm_ref, km_ref, vm_ref, nq_ref, nk_ref, nv_ref):
    x = x_ref[0]
    xb = _norm_mod(x, g_ref[...], mod_ref)[0].astype(BF16)
    y = _dot(xb, win_ref[...])
    lx_ref[0] = y[:, 0:LRU_W]
    lg_ref[0] = y[:, LRU_W:2 * LRU_W]
    cq = y[:, 768:1024]
    ckv = y[:, 1024:1152]
    kr_slot = y[:, 1152:1280]
    rc, rs = rc_ref[...], rs_ref[...]
    n_slot = MLA_H * HEAD_SLOT

    q2 = _dot((_rms(cq) * qn_ref[...]).astype(BF16), wuq_ref[...])
    q_raw = q2[:, :n_slot]
    inv_q = lax.rsqrt(_group_ssq(q_raw, ones_h_ref) * (1.0 / MLA_QK) + EPS)
    cos_q = rc * (mqg_ref[0:1] * MLA_QSCALE)
    sin_q = rs * (mqg_ref[1:2] * MLA_QSCALE)

    kv = _dot((_rms(ckv) * kvn_ref[...]).astype(BF16), wukv_ref[...])
    lane = lax.broadcasted_iota(jnp.int32, kr_slot.shape, 1)
    kr = jnp.where(lane < MLA_QK, kr_slot, 0.0)
    kr_partner = pltpu.roll(kr_slot, LANES - MLA_ROPE, 1)
    k_raw = kv[:, :n_slot] + jnp.concatenate([kr] * MLA_H, axis=1)
    inv_k = lax.rsqrt(_group_ssq(k_raw, ones_h_ref) * (1.0 / MLA_QK) + EPS)
    cos_k = rc * mkg_ref[0:1]
    sin_k = kr_partner * (rs * mkg_ref[1:2])
    for h in range(MLA_H):
        sl = slice(h * HEAD_SLOT, (h + 1) * HEAD_SLOT)
        qh = q_raw[:, sl] * cos_q + q2[:, n_slot + h * HEAD_SLOT:n_slot + (h + 1) * HEAD_SLOT] * sin_q
        qm_ref[0, :, sl] = (qh * inv_q[:, sl]).astype(BF16)
        km_ref[0, :, sl] = ((k_raw[:, sl] * cos_k + sin_k) * inv_k[:, sl]).astype(BF16)
    for s in range(GROUPS):
        vm_ref[0, s] = kv[s * TM:(s + 1) * TM, n_slot:].T.astype(BF16)

    nq = y[:, 1280:1536]
    nq_ref[0] = (nq * lax.rsqrt(_group_ssq(nq, ones_n_ref) * (1.0 / NA_DH) + EPS)
                 * (nqg_ref[...] * NA_QSCALE)).astype(BF16)
    nk = y[:, 1536:1792]
    nk_ref[0] = (nk * lax.rsqrt(_group_ssq(nk, ones_n_ref) * (1.0 / NA_DH) + EPS) * nkg_ref[...]).astype(BF16)
    for s in range(GROUPS):
        nv_ref[0, s] = y[s * TM:(s + 1) * TM, 1792:2048].T.astype(BF16)


def _tile_t_spec(rows):
    return pl.BlockSpec((1, GROUPS, rows, TM), lambda b, i: (b, i, 0, 0))


def _mixin_call(h, mod, p, tabs):
    widths = (LRU_W, LRU_W, MLA_H * HEAD_SLOT, MLA_H * HEAD_SLOT, NA_W, NA_W)
    dtypes = (F32, F32, BF16, BF16, BF16, BF16)
    rope_spec = pl.BlockSpec((TOK, LANES), lambda b, i: (i, 0))
    tok_shapes = [jax.ShapeDtypeStruct((B, TT, w), dt) for w, dt in zip(widths, dtypes)]
    out_specs = [_tok_spec(w) for w in widths]
    out_specs = out_specs[:4] + [_tile_t_spec(MLA_H * MLA_V)] + out_specs[4:] + [_tile_t_spec(NA_W)]
    out_shape = (tok_shapes[:4] + [jax.ShapeDtypeStruct((B, NT, MLA_H * MLA_V, TM), BF16)] + tok_shapes[4:]
                 + [jax.ShapeDtypeStruct((B, NT, NA_W, TM), BF16)])
    n_slot = MLA_H * HEAD_SLOT
    return pl.pallas_call(
        _mixin_kernel,
        grid=(B, NTOK),
        in_specs=[_tok_spec(D), _mod_spec(1), _const_spec((1, D)), _const_spec((D, IN_EXT)),
                  _const_spec((1, MLA_QR)), _const_spec((MLA_QR, 2 * n_slot)),
                  _const_spec((1, MLA_KVR)), _const_spec((MLA_KVR, n_slot + MLA_H * MLA_V)),
                  _const_spec((2, HEAD_SLOT)), _const_spec((2, HEAD_SLOT)),
                  _const_spec((1, NA_W)), _const_spec((1, NA_W)),
                  rope_spec, rope_spec, _const_spec((n_slot, n_slot)), _const_spec((NA_W, NA_W))],
        out_specs=out_specs,
        out_shape=out_shape,
        compiler_params=_cparams(("parallel", "arbitrary")),
        name="mix_in",
    )(h, mod, p["norm_mix"], p["w_in"], p["q_norm"], p["w_uq"], p["kv_norm"], p["w_ukv"],
      p["mq_gain"], p["mk_gain"], p["nq_gain"], p["nk_gain"], *tabs)


def _dot_tn(a, b):
    return lax.dot_general(a, b, (((0,), (0,)), ((), ())), preferred_element_type=F32)


def _mixout_kernel(h_ref, mod_ref, ylru_ref, ymla_ref, yna_ref, w1_ref, w2_ref, w3_ref, o_ref):
    att = jnp.concatenate([_dot_tn(ymla_ref[0, s], w2_ref[...]) + _dot_tn(yna_ref[0, s], w3_ref[...])
                           for s in range(GROUPS)], axis=0)
    y = _dot(ylru_ref[0].astype(BF16), w1_ref[...]) + att
    row = pl.program_id(1) * TOK + lax.broadcasted_iota(jnp.int32, (TOK, 1), 0)
    gate = jnp.where(row < CTX, mod_ref[0, 0, 0][2:3], mod_ref[0, 1, 0][2:3])
    o_ref[0] = h_ref[0] + gate * y


def _mixout_call(h, mod, ylru, ymla, yna, p):
    return pl.pallas_call(
        _mixout_kernel,
        grid=(B, NTOK),
        in_specs=[_tok_spec(D), _mod_spec(1), _tok_spec(LRU_W), _tile_t_spec(MLA_H * MLA_V), _tile_t_spec(NA_W),
                  _const_spec((LRU_W, D)), _const_spec((MLA_H * MLA_V, D)), _const_spec((NA_W, D))],
        out_specs=_tok_spec(D),
        out_shape=jax.ShapeDtypeStruct((B, TT, D), F32),
        compiler_params=_cparams(("parallel", "arbitrary")),
        name="mix_out",
    )(h, mod, ylru, ymla, yna, p["w_out_lru"], p["w_out_mla"], p["w_out_na"])


LRU_CHUNK = 256
LRU_PAD = 8
LRU_XROWS = TT + 3 * LRU_PAD
SUB = 8


def _scan_block(a, u, carry, reverse):
    row = lax.broadcasted_iota(jnp.int32, a.shape, 0)
    for s in (1, 2, 4):
        if reverse:
            keep = row < SUB - s
            a_s = jnp.where(keep, pltpu.roll(a, SUB - s, 0), 1.0)
            u_s = jnp.where(keep, pltpu.roll(u, SUB - s, 0), 0.0)
        else:
            keep = row >= s
            a_s = jnp.where(keep, pltpu.roll(a, s, 0), 1.0)
            u_s = jnp.where(keep, pltpu.roll(u, s, 0), 0.0)
        u = a * u_s + u
        a = a * a_s
    h = a * carry + u
    edge = h[0:1] if reverse else h[SUB - 1:SUB]
    return h, jnp.broadcast_to(edge, a.shape)


LRU_SLICE = 64
LRU_NCHUNK = TT // LRU_CHUNK


def _lru_kernel(lx_ref, lg_ref, cw_ref, cb_ref, wg_ref, bg_ref, lam_ref, y_ref, xpad, hf, hb, au0, au1):
    zeros_pad = jnp.zeros((LRU_PAD, LANES), F32)
    xpad[0:LRU_PAD] = zeros_pad
    xpad[LRU_PAD:LRU_PAD + CTX] = lx_ref[0, 0:CTX]
    xpad[LRU_PAD + CTX:2 * LRU_PAD + CTX] = zeros_pad
    xpad[2 * LRU_PAD + CTX:2 * LRU_PAD + TT] = lx_ref[0, CTX:TT]
    xpad[2 * LRU_PAD + TT:3 * LRU_PAD + TT] = zeros_pad

    neg_c_softplus = -LRU_C * jax.nn.softplus(-lam_ref[...])
    n_slices = LRU_CHUNK // LRU_SLICE
    blocks_per_slice = LRU_SLICE // SUB

    def fwd_chunk(i):
        return i

    def bwd_chunk(i):
        return jnp.where(i == 0, 0, LRU_NCHUNK - i)

    def gates(c, d, s, buf):
        base = pl.multiple_of(jnp.where(c == 0, LRU_PAD, 2 * LRU_PAD + c * LRU_CHUNK), SUB) + s * LRU_SLICE
        xc = jnp.broadcast_to(cb_ref[...], (LRU_SLICE, LANES))
        for j in range(CONV_W):
            xc = xc + xpad[pl.ds(base + (j - CONV_LEFT), LRU_SLICE), :] * cw_ref[j:j + 1, :]
        cols = slice(d * 2 * LANES, (d + 1) * 2 * LANES)
        pre = _dot(xc.astype(BF16), wg_ref[0, :, cols]) + bg_ref[0, :, cols]
        r = jax.nn.sigmoid(pre[:, :LANES])
        gate_in = jax.nn.sigmoid(pre[:, LANES:])
        log_a = r * neg_c_softplus[d:d + 1]
        rows = slice(s * LRU_SLICE, (s + 1) * LRU_SLICE)
        buf[2 * d, rows, :] = jnp.exp(log_a)
        th = jnp.tanh(log_a)
        buf[2 * d + 1, rows, :] = jnp.sqrt(-2.0 * th / (1.0 - th)) * (gate_in * xc)

    def scan(c, d, s, buf, carry, dst):
        row0 = pl.multiple_of(c * LRU_CHUNK, SUB)
        for k in range(blocks_per_slice):
            blk = s * blocks_per_slice + k
            if d == 1:
                blk = LRU_CHUNK // SUB - 1 - blk
            rows = slice(blk * SUB, (blk + 1) * SUB)
            h, carry = _scan_block(buf[2 * d, rows, :], buf[2 * d + 1, rows, :], carry, d == 1)
            dst[pl.ds(row0 + blk * SUB, SUB), :] = h
        return carry

    def step(i, cur, nxt, carry):
        cf, cb = carry
        for s in range(n_slices):
            if nxt is not None:
                gates(fwd_chunk(i + 1), 0, s, nxt)
                gates(bwd_chunk(i + 1), 1, s, nxt)
            cf = scan(fwd_chunk(i), 0, s, cur, cf, hf)
            cb = scan(bwd_chunk(i), 1, s, cur, cb, hb)
        return cf, cb

    for s in range(n_slices):
        gates(0, 0, s, au0)
        gates(0, 1, s, au0)

    def pair(j, carry):
        carry = step(2 * j, au0, au1, carry)
        return step(2 * j + 1, au1, au0, carry)

    assert LRU_NCHUNK % 2 == 1
    zero = jnp.zeros((SUB, LANES), F32)
    carry = lax.fori_loop(0, LRU_NCHUNK // 2, pair, (zero, zero))
    step(LRU_NCHUNK - 1, au0, None, carry)

    def finish(c, _):
        rows = pl.ds(pl.multiple_of(c * LRU_CHUNK, LRU_CHUNK), LRU_CHUNK)
        y_ref[0, rows, :] = ((hf[rows, :] + hb[rows, :]) * jax.nn.gelu(lg_ref[0, rows, :])).astype(y_ref.dtype)
        return 0

    lax.fori_loop(0, LRU_NCHUNK, finish, 0)


def _lru_call(lx, lg, p):
    nl = LRU_W // LANES
    seq_spec = pl.BlockSpec((1, TT, LANES), lambda b, j: (b, 0, j))
    return pl.pallas_call(
        _lru_kernel,
        grid=(B, nl),
        in_specs=[seq_spec, seq_spec,
                  pl.BlockSpec((CONV_W, LANES), lambda b, j: (0, j)),
                  pl.BlockSpec((1, LANES), lambda b, j: (0, j)),
                  pl.BlockSpec((1, LANES, 4 * LANES), lambda b, j: (j, 0, 0)),
                  pl.BlockSpec((1, 1, 4 * LANES), lambda b, j: (j, 0, 0)),
                  pl.BlockSpec((2, LANES), lambda b, j: (0, j))],
        out_specs=seq_spec,
        out_shape=jax.ShapeDtypeStruct((B, TT, LRU_W), BF16),
        scratch_shapes=[pltpu.VMEM((LRU_XROWS, LANES), F32), pltpu.VMEM((TT, LANES), F32),
                        pltpu.VMEM((TT, LANES), F32),
                        pltpu.VMEM((4, LRU_CHUNK, LANES), F32), pltpu.VMEM((4, LRU_CHUNK, LANES), F32)],
        compiler_params=_cparams(("parallel", "arbitrary")),
        name="lru",
    )(lx, lg, p["conv_w"], p["conv_b"], p["lru_wg"], p["lru_bg"], p["lru_lam"])


MLA_TQ = 1024
MLA_TK = 256
MLA_NKV = TT // MLA_TK
MLA_UNROLL = 8
MLA_QCOL = 256


def _with_ones(vt):
    return jnp.concatenate([vt, jnp.ones((V_ROWS - vt.shape[0], vt.shape[1]), vt.dtype)], axis=0)


def _softmax_pv(s, vt, m, acc):
    m_new = jnp.maximum(m, jnp.max(s, axis=0, keepdims=True))
    p = jnp.exp2(s - m_new).astype(BF16)
    return m_new, acc * jnp.exp2(m - m_new) + _dot(vt, p)


MLA_NCOL = MLA_TQ // MLA_QCOL
MLA_COLS = [slice(c * MLA_QCOL, (c + 1) * MLA_QCOL) for c in range(MLA_NCOL)]


def _attend(q, q_next, k_ref, vt_ref, cur, nxt):
    tk = MLA_TK
    assert MLA_NKV % MLA_UNROLL == 1 and MLA_UNROLL % 2 == 0
    q_cols = [q[sl] for sl in MLA_COLS]

    def keys(i):
        return k_ref[0, pl.ds(pl.multiple_of(i * tk, tk), tk), :]

    def values(i):
        return _with_ones(vt_ref[0, i])

    def group(j, carry):
        ms, accs = list(carry[0]), list(carry[1])
        a, b = cur, nxt
        for u in range(MLA_UNROLL):
            i = MLA_UNROLL * j + u
            k_next, v_cur = keys(i + 1), values(i)
            for c, sl in enumerate(MLA_COLS):
                b[:, sl] = _dot_nt(k_next, q_cols[c])
                ms[c], accs[c] = _softmax_pv(a[:, sl], v_cur, ms[c], accs[c])
            a, b = b, a
        return tuple(ms), tuple(accs)

    init = (tuple(jnp.full((1, MLA_QCOL), NEG, F32) for _ in MLA_COLS),
            tuple(jnp.zeros((V_ROWS, MLA_QCOL), F32) for _ in MLA_COLS))
    ms, accs = lax.fori_loop(0, MLA_NKV // MLA_UNROLL, group, init)
    k_first, v_last = k_ref[0, 0:tk, :], values(MLA_NKV - 1)
    outs = []
    for c, sl in enumerate(MLA_COLS):
        nxt[:, sl] = _dot_nt(k_first, q_next[sl])
        _, acc = _softmax_pv(cur[:, sl], v_last, ms[c], accs[c])
        outs.append(acc[:MLA_V] / acc[MLA_V:MLA_V + 1])
    return jnp.concatenate(outs, axis=1)


def _mla_kernel(q_ref, k_ref, vt_ref, o_ref, s_a, s_b):
    qc = q_ref[0, 0:CTX, :]
    sc = _dot_nt(k_ref[0, 0:CTX, :], qc)
    _, acc = _softmax_pv(sc, _with_ones(vt_ref[0, 0]), jnp.full((1, CTX), NEG, F32),
                         jnp.zeros((V_ROWS, CTX), F32))
    o_ref[0, 0] = (acc[:MLA_V] / acc[MLA_V:MLA_V + 1]).astype(o_ref.dtype)

    n_tiles = SEQ // MLA_TQ
    assert n_tiles % 2 == 0

    def q_tile(t):
        t = jnp.minimum(t, n_tiles - 1)
        return q_ref[0, pl.ds(pl.multiple_of(CTX + t * MLA_TQ, TM), MLA_TQ), :]

    def store(t, o):
        for g in range(MLA_TQ // TM):
            o_ref[0, 1 + t * (MLA_TQ // TM) + g] = o[:, g * TM:(g + 1) * TM].astype(o_ref.dtype)

    q0 = q_tile(0)
    for sl in MLA_COLS:
        s_a[:, sl] = _dot_nt(k_ref[0, 0:MLA_TK, :], q0[sl])

    def tile_pair(tt, _):
        t = 2 * tt
        store(t, _attend(q_tile(t), q_tile(t + 1), k_ref, vt_ref, s_a, s_b))
        store(t + 1, _attend(q_tile(t + 1), q_tile(t + 2), k_ref, vt_ref, s_b, s_a))
        return 0

    lax.fori_loop(0, n_tiles // 2, tile_pair, 0)


def _mla_call(qm, km, vt):
    assert MLA_TK == TM
    head_t_spec = pl.BlockSpec((1, NT, MLA_V, TM), lambda b, h: (b, 0, h, 0))
    return pl.pallas_call(
        _mla_kernel,
        grid=(B, MLA_H),
        in_specs=[pl.BlockSpec((1, TT, HEAD_SLOT), lambda b, h: (b, 0, h)),
                  pl.BlockSpec((1, TT, HEAD_SLOT), lambda b, h: (b, 0, h)),
                  head_t_spec],
        out_specs=head_t_spec,
        out_shape=jax.ShapeDtypeStruct((B, NT, MLA_H * MLA_V, TM), BF16),
        scratch_shapes=[pltpu.VMEM((MLA_TK, MLA_TQ), F32), pltpu.VMEM((MLA_TK, MLA_TQ), F32)],
        compiler_params=_cparams(("parallel", "arbitrary")),
        name="mla_attn",
    )(qm, km, vt)


def _na_kernel(q_ref, k_ref, vt_ref, tab_ref, o_ref):
    j = pl.program_id(1)
    q_all = q_ref[0]
    lane_head = jnp.right_shift(lax.broadcasted_iota(jnp.int32, q_all.shape, 1), 6)
    q_heads = [jnp.where(lane_head == h, q_all, jnp.zeros_like(q_all)) for h in range(NA_H)]

    def values(g, h):
        return vt_ref[0, g, h * NA_DH:(h + 1) * NA_DH, :]

    def finish(h, s, vt):
        m = jnp.max(s, axis=0, keepdims=True)
        acc = _dot(_with_ones(vt), jnp.exp(s - m).astype(BF16))
        o_ref[0, 0, h * NA_DH:(h + 1) * NA_DH, :] = (acc[:NA_DH] / acc[NA_DH:NA_DH + 1]).astype(o_ref.dtype)

    @pl.when(j == 0)
    def _():
        for h in range(NA_H):
            finish(h, _dot_nt(k_ref[0, 0], q_heads[h]), values(0, h))

    @pl.when(j > 0)
    def _():
        jj = j - 1
        g0 = jnp.clip(jj - 1, 0, NA_NGROUPS - NA_BAND_G)
        variant = jnp.where(jj == 0, 0, jnp.where(jj == NA_NGROUPS - 1, 2, 1))
        band = k_ref[0, pl.ds(1 + g0, NA_BAND_G)].reshape(NA_BAND_G * TM, NA_W)
        keys = jnp.concatenate([k_ref[0, 0], band], axis=0)
        scores, probs = {}, {}
        for t in range(NA_H + 2):
            if t < NA_H:
                scores[t] = _dot_nt(keys, q_heads[t])
            h = t - 1
            if 0 <= h < NA_H:
                s = scores.pop(h)
                s = jnp.concatenate([s[:TM], s[TM:] + tab_ref[variant, h]], axis=0)
                probs[h] = jnp.exp(s - jnp.max(s, axis=0, keepdims=True)).astype(BF16)
            h = t - 2
            if 0 <= h < NA_H:
                vt = jnp.concatenate([values(0, h)] + [values(1 + g0 + g, h) for g in range(NA_BAND_G)], axis=1)
                acc = _dot(_with_ones(vt), probs.pop(h))
                o_ref[0, 0, h * NA_DH:(h + 1) * NA_DH, :] = (
                    acc[:NA_DH] / acc[NA_DH:NA_DH + 1]).astype(o_ref.dtype)


def _na_call(q, k, vt, tab, layer):
    whole = pl.BlockSpec((1, NT, TM, NA_W), lambda b, j: (b, 0, 0, 0))
    return pl.pallas_call(
        _na_kernel,
        grid=(B, NT),
        in_specs=[pl.BlockSpec((1, TM, NA_W), lambda b, j: (b, j, 0)), whole, whole,
                  _layer_spec((3, NA_H, NA_BAND_G * TM, TM), layer)],
        out_specs=pl.BlockSpec((1, 1, NA_W, TM), lambda b, j: (b, j, 0, 0)),
        out_shape=jax.ShapeDtypeStruct((B, NT, NA_W, TM), BF16),
        compiler_params=_cparams(("parallel", "arbitrary")),
        name="na_attn",
    )(q, k, vt, tab)


def _rope_tables():
    t = jnp.arange(SEQ)
    pos = jnp.stack([t // GRID_W, t % GRID_W], axis=-1).astype(F32)
    half = MLA_ROPE // 2
    inv = ROPE_BASE ** (-jnp.arange(0, half, 2, dtype=F32) / half)
    ang = pos[:, :, None] * inv
    cos, sin = jnp.cos(ang), jnp.sin(ang)
    ones = jnp.ones((SEQ, MLA_NOPE), F32)
    c = jnp.concatenate([ones, cos[:, 0], cos[:, 0], cos[:, 1], cos[:, 1], ones[:, :32]], axis=-1)
    s = jnp.concatenate([0 * ones, -sin[:, 0], sin[:, 0], -sin[:, 1], sin[:, 1], 0 * ones[:, :32]], axis=-1)
    ctx_c = jnp.ones((CTX, LANES), F32)
    ctx_s = jnp.zeros((CTX, LANES), F32)
    return jnp.concatenate([ctx_c, c]), jnp.concatenate([ctx_s, s])


ROPE_PARTNER = np.concatenate([np.arange(8, 16), np.arange(0, 8), np.arange(24, 32), np.arange(16, 24)])


def _group_ones(width, group):
    g = np.arange(width) // group
    return jnp.asarray(g[:, None] == g[None, :], BF16)


def _na_window_pattern():
    n_band = NA_BAND_G * NA_QROWS
    i = np.arange(n_band)[:, None]
    a = np.arange(NA_QROWS)[None, :]

    def pattern(jj):
        g0 = min(max(jj - 1, 0), NA_NGROUPS - NA_BAND_G)
        krow = NA_QROWS * g0 + i
        qrow = NA_QROWS * jj + a
        r0q = np.clip(qrow - NA_WR // 2, 0, GRID_H - NA_WR)
        return (krow >= r0q) & (krow < r0q + NA_WR), krow - qrow + NA_WR - 1

    interior = [pattern(jj) for jj in range(1, NA_NGROUPS - 1)]
    assert all((v == interior[0][0]).all() and (r[v] == interior[0][1][v]).all() for v, r in interior)
    pats = [pattern(0), interior[0], pattern(NA_NGROUPS - 1)]
    valid = np.stack([p[0] for p in pats])
    row_off = np.stack([p[1] for p in pats])
    assert (row_off[valid] >= 0).all() and (row_off[valid] <= 2 * NA_WR - 2).all()
    return valid, row_off


def _na_tables(rpb):
    kc = jnp.arange(GRID_W)[:, None]
    qc = jnp.arange(GRID_W)[None, :]
    win = jnp.clip(qc - NA_WC // 2, 0, GRID_W - NA_WC)
    ok_col = (kc >= win) & (kc < win + NA_WC)
    col_off = kc - qc + NA_WC - 1
    pick_col = ((col_off[None] == jnp.arange(2 * NA_WC - 1)[:, None, None]) & ok_col[None]).astype(F32)
    valid, row_off = _na_window_pattern()
    pick_row = ((row_off[..., None] == jnp.arange(2 * NA_WR - 1)) & valid[..., None]).astype(F32)
    tab = jnp.einsum("vian,dhnj,jkq->dvhikaq", pick_row, rpb, pick_col, precision=lax.Precision.HIGHEST)
    keep = valid[:, None, :, None, :, None] & ok_col[None, None, None, :, None, :]
    tab = jnp.where(keep, tab, NEG)
    return tab.reshape(DEPTH, 3, NA_H, NA_BAND_G * TM, TM)


def _block_diag(w):
    w = w.reshape(DEPTH, 2, LRU_W // LANES, 2, LRU_BW, LRU_BW)
    z = jnp.zeros_like(w[:, :, :, 0])
    top = jnp.concatenate([w[:, :, :, 0], z], axis=-1)
    bot = jnp.concatenate([z, w[:, :, :, 1]], axis=-1)
    return jnp.concatenate([top, bot], axis=-2)


def _prepare(w_in, w_out, lru_conv_b, lru_w_a, lru_b_a, lru_w_x, lru_b_x, mla_w_uq, mla_w_ukv,
             mla_q_gain, mla_k_gain, na_q_gain, na_k_gain):
    z = lambda n: jnp.zeros((DEPTH, D, n), F32)
    kr = w_in[..., 1152:1184]
    w_in_ext = jnp.concatenate(
        [w_in[..., :1152], z(MLA_NOPE), kr, kr[..., ROPE_PARTNER], w_in[..., 1184:]], axis=-1).astype(BF16)
    w_uq = mla_w_uq.reshape(DEPTH, MLA_QR, MLA_H, MLA_QK)
    slot_pad = ((0, 0), (0, 0), (0, 0), (0, HEAD_SLOT - MLA_QK))
    zq = jnp.zeros((DEPTH, MLA_QR, MLA_H, MLA_NOPE), F32)
    w_uq_partner = jnp.concatenate([zq, w_uq[..., MLA_NOPE + ROPE_PARTNER]], axis=-1)
    w_uq = jnp.concatenate([jnp.pad(w_uq, slot_pad).reshape(DEPTH, MLA_QR, -1),
                            jnp.pad(w_uq_partner, slot_pad).reshape(DEPTH, MLA_QR, -1)], axis=-1)
    w_ukv = mla_w_ukv.reshape(DEPTH, MLA_KVR, MLA_H, MLA_NOPE + MLA_V)
    w_uk = jnp.pad(w_ukv[..., :MLA_NOPE], ((0, 0), (0, 0), (0, 0), (0, HEAD_SLOT - MLA_NOPE)))
    w_ukv_ext = jnp.concatenate([w_uk.reshape(DEPTH, MLA_KVR, -1),
                                 w_ukv[..., MLA_NOPE:].reshape(DEPTH, MLA_KVR, -1)], axis=-1)

    def pad_gain(g):
        partner = jnp.concatenate([jnp.zeros((DEPTH, MLA_NOPE), F32), g[:, MLA_NOPE + ROPE_PARTNER]], axis=-1)
        both = jnp.stack([g, partner], axis=1)
        return jnp.pad(both, ((0, 0), (0, 0), (0, HEAD_SLOT - MLA_QK)))
    tile_gain = lambda g: jnp.tile(g, (1, NA_H)).reshape(DEPTH, 1, NA_W)
    wa, wx = _block_diag(lru_w_a), _block_diag(lru_w_x)
    wg = jnp.concatenate([wa[:, 0], wx[:, 0], wa[:, 1], wx[:, 1]], axis=-1).astype(BF16)
    ba = lru_b_a.reshape(DEPTH, 2, LRU_W // LANES, 1, LANES)
    bx = lru_b_x.reshape(DEPTH, 2, LRU_W // LANES, 1, LANES)
    bg = jnp.concatenate([ba[:, 0], bx[:, 0], ba[:, 1], bx[:, 1]], axis=-1)
    w_out = w_out.astype(BF16)
    return dict(
        w_in=w_in_ext, w_uq=w_uq.astype(BF16), w_ukv=w_ukv_ext.astype(BF16),
        mq_gain=pad_gain(mla_q_gain), mk_gain=pad_gain(mla_k_gain),
        nq_gain=tile_gain(na_q_gain), nk_gain=tile_gain(na_k_gain),
        lru_wg=wg, lru_bg=bg, conv_b=lru_conv_b.reshape(DEPTH, 1, LRU_W),
        w_out_lru=w_out[:, :LRU_W], w_out_mla=w_out[:, LRU_W:LRU_W + MLA_H * MLA_V],
        w_out_na=w_out[:, LRU_W + MLA_H * MLA_V:],
    )


def kernel(x, c, ctx, c_ctx, w_mod, b_mod, norm_ffn1, ffn1_w_in, ffn1_w_out, norm_mix, w_in, w_out, lru_conv_w, lru_conv_b, lru_w_a, lru_b_a, lru_w_x, lru_b_x, lru_lambda, mla_q_norm, mla_w_uq, mla_kv_norm, mla_w_ukv, mla_q_gain, mla_k_gain, na_q_gain, na_k_gain, na_rpb, norm_ffn2, ffn2_w_in, ffn2_w_out):
    cc = jnp.concatenate([c, c_ctx[None], jnp.zeros((8 - B - 1, D), F32)], axis=0)
    mod = _mod_call(cc, w_mod, b_mod)
    mod = mod.reshape(DEPTH, 8, 3, 3, D)
    mod = jnp.stack([jnp.broadcast_to(mod[:, B:B + 1], (DEPTH, B, 3, 3, D)), mod[:, :B]], axis=2)

    prm = _prepare(w_in, w_out, lru_conv_b, lru_w_a, lru_b_a, lru_w_x, lru_b_x, mla_w_uq, mla_w_ukv,
                   mla_q_gain, mla_k_gain, na_q_gain, na_k_gain)
    prm.update(norm_mix=norm_mix.reshape(DEPTH, 1, D), q_norm=mla_q_norm.reshape(DEPTH, 1, MLA_QR),
               kv_norm=mla_kv_norm.reshape(DEPTH, 1, MLA_KVR), conv_w=lru_conv_w, lru_lam=lru_lambda)
    f1_in, f1_out = ffn1_w_in.astype(BF16), ffn1_w_out.astype(BF16)
    f2_in, f2_out = ffn2_w_in.astype(BF16), ffn2_w_out.astype(BF16)
    g1, g2 = norm_ffn1.reshape(DEPTH, 1, D), norm_ffn2.reshape(DEPTH, 1, D)
    rope_tabs = _rope_tables() + (_group_ones(MLA_H * HEAD_SLOT, HEAD_SLOT), _group_ones(NA_W, NA_DH))
    na_tabs = _na_tables(na_rpb)

    h = jnp.concatenate([ctx, x], axis=1)
    for l in range(DEPTH):
        p = {k: v[l] for k, v in prm.items()}
        h = _ffn_call(h, mod[l], g1[l], f1_in, f1_out, l, 0, "ffn1")
        lx, lg, qm, km, vm, nq, nk, nv = _mixin_call(h, mod[l], p, rope_tabs)
        ylru = _lru_call(lx, lg, p)
        ymla = _mla_call(qm, km, vm)
        yna = _na_call(nq, nk.reshape(B, NT, TM, NA_W), nv, na_tabs, l)
        h = _mixout_call(h, mod[l], ylru, ymla, yna, p)
        h = _ffn_call(h, mod[l], g2[l], f2_in, f2_out, l, 2, "ffn2")
    return h[:, CTX:]
```

```python
import functools
import math

import jax
import jax.numpy as jnp
import numpy as np
from jax import lax
from jax.experimental import pallas as pl
from jax.experimental.pallas import tpu as pltpu

F32 = jnp.float32
BF16 = jnp.bfloat16

D = 1024
B = 4
SEQ = 8192
DEPTH = 4
GRID_W = 64
GRID_H = SEQ // GRID_W
CTX = 256
TT = CTX + SEQ
D_FF = 2816
EPS = 1e-6

LRU_W = 384
LRU_BW = 64
LRU_C = 8.0
CONV_W = 4
CONV_LEFT = 2

MLA_H = 6
MLA_QR = 256
MLA_KVR = 128
MLA_NOPE = 64
MLA_ROPE = 32
MLA_QK = MLA_NOPE + MLA_ROPE
MLA_V = 64
ROPE_BASE = 10000.0

NA_H = 4
NA_DH = 64
NA_W = NA_H * NA_DH
NA_WR = 8
NA_WC = 16

LANES = 128
HEAD_SLOT = LANES
IN_EXT = 2048
TM = 256
NT = TT // TM
V_ROWS = 80
NEG = -1e30
VMEM_LIMIT = 56 * 1024 * 1024

NA_QROWS = 4
NA_GROUP = NA_QROWS * GRID_W
NA_NGROUPS = SEQ // NA_GROUP
NA_BAND_G = 3

MLA_QSCALE = MLA_QK ** -0.5 * math.log2(math.e)
NA_QSCALE = NA_DH ** -0.5


def _cparams(sem):
    return pltpu.CompilerParams(dimension_semantics=sem, vmem_limit_bytes=VMEM_LIMIT)


def _dot(a, b):
    return jnp.dot(a, b, preferred_element_type=F32)


def _dot_nt(a, b):
    return lax.dot_general(a, b, (((1,), (1,)), ((), ())), preferred_element_type=F32)


def _rms(x, n=None):
    n = x.shape[-1] if n is None else n
    return x * lax.rsqrt(jnp.sum(x * x, axis=-1, keepdims=True) * (1.0 / n) + EPS)


def _mod_kernel(c_ref, w_ref, b_ref, o_ref):
    s = jax.nn.silu(c_ref[...]).astype(BF16)
    o_ref[0] = _dot(s, w_ref[0].astype(BF16)) + b_ref[0]


def _mod_call(cc, w_mod, b_mod):
    tn = 1024
    n = w_mod.shape[-1]
    return pl.pallas_call(
        _mod_kernel,
        grid=(DEPTH, n // tn),
        in_specs=[
            pl.BlockSpec((8, D), lambda l, j: (0, 0)),
            pl.BlockSpec((1, D, tn), lambda l, j: (l, 0, j)),
            pl.BlockSpec((1, 1, tn), lambda l, j: (l, 0, j)),
        ],
        out_specs=pl.BlockSpec((1, 8, tn), lambda l, j: (l, 0, j)),
        out_shape=jax.ShapeDtypeStruct((DEPTH, 8, n), F32),
        compiler_params=_cparams(("arbitrary", "arbitrary")),
        name="mod",
    )(cc, w_mod, b_mod.reshape(DEPTH, 1, n))


TOK = 768
GROUPS = TOK // TM
NTOK = TT // TOK


def _norm_mod(x, g, mod_ref):
    row = pl.program_id(1) * TOK + lax.broadcasted_iota(jnp.int32, (x.shape[0], 1), 0)
    is_ctx = row < CTX
    mod_c, mod_l = mod_ref[0, 0, 0], mod_ref[0, 1, 0]
    pick = lambda r: jnp.where(is_ctx, mod_c[r:r + 1], mod_l[r:r + 1])
    xn = _rms(x) * g
    return xn * (1.0 + pick(1)) + pick(0), pick(2)


MXU_TILE = 256
FFN_SPLIT = (D_FF // MXU_TILE + 1) // 2 * MXU_TILE
FFN_CHUNKS = ((0, FFN_SPLIT), (FFN_SPLIT, D_FF))


def _ffn_body(x, mod_ref, g_ref, win_ref, wout_ref, chunks):
    xm, gate_row = _norm_mod(x, g_ref[...], mod_ref)
    xb = xm.astype(BF16)
    acc = jnp.zeros((x.shape[0], D), F32)
    for lo, hi in chunks:
        gate = _dot(xb, win_ref[:, lo:hi])
        up = _dot(xb, win_ref[:, D_FF + lo:D_FF + hi])
        a = (jax.nn.silu(gate) * up).astype(BF16)
        acc = acc + _dot(a, wout_ref[lo:hi, :])
    return x + 0.5 * gate_row * acc


def _ffn_kernel(x_ref, mod_ref, g_ref, win_ref, wout_ref, o_ref):
    o_ref[0] = _ffn_body(x_ref[0], mod_ref, g_ref, win_ref, wout_ref, FFN_CHUNKS)


def _mod_spec(group):
    return pl.BlockSpec((1, 2, 1, 3, D), lambda b, i: (b, 0, group, 0, 0))


def _tok_spec(width):
    return pl.BlockSpec((1, TOK, width), lambda b, i: (b, i, 0))


def _const_spec(shape):
    return pl.BlockSpec(shape, lambda b, i: (0,) * len(shape), pipeline_mode=pl.Buffered(1))


def _layer_spec(shape, layer):
    return pl.BlockSpec((None,) + shape, lambda b, i: (layer,) + (0,) * len(shape),
                        pipeline_mode=pl.Buffered(1))


def _ffn_call(h, mod, g, w_in, w_out, layer, group, name):
    return pl.pallas_call(
        _ffn_kernel,
        grid=(B, NTOK),
        in_specs=[_tok_spec(D), _mod_spec(group), _const_spec((1, D)),
                  _layer_spec((D, 2 * D_FF), layer), _layer_spec((D_FF, D), layer)],
        out_specs=_tok_spec(D),
        out_shape=jax.ShapeDtypeStruct((B, TT, D), F32),
        compiler_params=_cparams(("parallel", "arbitrary")),
        name=name,
    )(h, mod, g, w_in, w_out)


def _group_ssq(x, ones_ref):
    sq = x * x
    hi = sq.astype(BF16)
    lo = (sq - hi.astype(F32)).astype(BF16)
    ones = ones_ref[...]
    tiles = [slice(c, c + MXU_TILE) for c in range(0, x.shape[1], MXU_TILE)]
    return jnp.concatenate([_dot(hi[:, t], ones) + _dot(lo[:, t], ones) for t in tiles], axis=1)


def _mixin_kernel(x_ref, mod_ref, g_ref, win_ref, qn_ref, wuq_ref, kvn_ref, wukv_ref,
                  mqg_ref, mkg_ref, nqg_ref, nkg_ref, rc_ref, rs_ref, ones_h_ref, ones_n_ref,
                  lx_ref, lg_ref, qm_ref, km_ref, vm_ref, nq_ref, nk_ref, nv_ref):
    x = x_ref[0]
    xb = _norm_mod(x, g_ref[...], mod_ref)[0].astype(BF16)
    y = _dot(xb, win_ref[...])
    lx_ref[0] = y[:, 0:LRU_W]
    lg_ref[0] = y[:, LRU_W:2 * LRU_W]
    cq = y[:, 768:1024]
    ckv = y[:, 1024:1152]
    kr_slot = y[:, 1152:1280]
    rc, rs = rc_ref[...], rs_ref[...]
    n_slot = MLA_H * HEAD_SLOT

    q2 = _dot((_rms(cq) * qn_ref[...]).astype(BF16), wuq_ref[...])
    q_raw = q2[:, :n_slot]
    inv_q = lax.rsqrt(_group_ssq(q_raw, ones_h_ref) * (1.0 / MLA_QK) + EPS)
    cos_q = rc * (mqg_ref[0:1] * MLA_QSCALE)
    sin_q = rs * (mqg_ref[1:2] * MLA_QSCALE)

    kv = _dot((_rms(ckv) * kvn_ref[...]).astype(BF16), wukv_ref[...])
    lane = lax.broadcasted_iota(jnp.int32, kr_slot.shape, 1)
    kr = jnp.where(lane < MLA_QK, kr_slot, 0.0)
    kr_partner = pltpu.roll(kr_slot, LANES - MLA_ROPE, 1)
    k_raw = kv[:, :n_slot] + jnp.concatenate([kr] * MLA_H, axis=1)
    inv_k = lax.rsqrt(_group_ssq(k_raw, ones_h_ref) * (1.0 / MLA_QK) + EPS)
    cos_k = rc * mkg_ref[0:1]
    sin_k = kr_partner * (rs * mkg_ref[1:2])
    q_heads = []
    for h in range(MLA_H):
        sl = slice(h * HEAD_SLOT, (h + 1) * HEAD_SLOT)
        qh = q_raw[:, sl] * cos_q + q2[:, n_slot + h * HEAD_SLOT:n_slot + (h + 1) * HEAD_SLOT] * sin_q
        q_heads.append(qh * inv_q[:, sl])
        km_ref[0, :, sl] = ((k_raw[:, sl] * cos_k + sin_k) * inv_k[:, sl]).astype(BF16)
    q_all = jnp.concatenate(q_heads, axis=1)
    for s in range(GROUPS):
        qm_ref[0, s] = q_all[s * TM:(s + 1) * TM].T.astype(BF16)
        vm_ref[0, s] = kv[s * TM:(s + 1) * TM, n_slot:].T.astype(BF16)

    nq = y[:, 1280:1536]
    nq = nq * lax.rsqrt(_group_ssq(nq, ones_n_ref) * (1.0 / NA_DH) + EPS) * (nqg_ref[...] * NA_QSCALE)
    nk = y[:, 1536:1792]
    nk_ref[0] = (nk * lax.rsqrt(_group_ssq(nk, ones_n_ref) * (1.0 / NA_DH) + EPS) * nkg_ref[...]).astype(BF16)
    for s in range(GROUPS):
        nq_ref[0, s] = nq[s * TM:(s + 1) * TM].T.astype(BF16)
        nv_ref[0, s] = y[s * TM:(s + 1) * TM, 1792:2048].T.astype(BF16)


def _tile_t_spec(rows):
    return pl.BlockSpec((1, GROUPS, rows, TM), lambda b, i: (b, i, 0, 0))


def _mixin_call(h, mod, p, tabs):
    n_slot = MLA_H * HEAD_SLOT
    tok = lambda w, dt: (_tok_spec(w), jax.ShapeDtypeStruct((B, TT, w), dt))
    tile_t = lambda rows: (_tile_t_spec(rows), jax.ShapeDtypeStruct((B, NT, rows, TM), BF16))
    outs = [tok(LRU_W, F32), tok(LRU_W, F32), tile_t(n_slot), tok(n_slot, BF16), tile_t(MLA_H * MLA_V),
            tile_t(NA_W), tok(NA_W, BF16), tile_t(NA_W)]
    out_specs, out_shape = [o[0] for o in outs], [o[1] for o in outs]
    rope_spec = pl.BlockSpec((TOK, LANES), lambda b, i: (i, 0))
    return pl.pallas_call(
        _mixin_kernel,
        grid=(B, NTOK),
        in_specs=[_tok_spec(D), _mod_spec(1), _const_spec((1, D)), _const_spec((D, IN_EXT)),
                  _const_spec((1, MLA_QR)), _const_spec((MLA_QR, 2 * n_slot)),
                  _const_spec((1, MLA_KVR)), _const_spec((MLA_KVR, n_slot + MLA_H * MLA_V)),
                  _const_spec((2, HEAD_SLOT)), _const_spec((2, HEAD_SLOT)),
                  _const_spec((1, NA_W)), _const_spec((1, NA_W)),
                  rope_spec, rope_spec, _const_spec((MXU_TILE, MXU_TILE)), _const_spec((MXU_TILE, MXU_TILE))],
        out_specs=out_specs,
        out_shape=out_shape,
        compiler_params=_cparams(("parallel", "arbitrary")),
        name="mix_in",
    )(h, mod, p["norm_mix"], p["w_in"], p["q_norm"], p["w_uq"], p["kv_norm"], p["w_ukv"],
      p["mq_gain"], p["mk_gain"], p["nq_gain"], p["nk_gain"], *tabs)


def _dot_tn(a, b):
    return lax.dot_general(a, b, (((0,), (0,)), ((), ())), preferred_element_type=F32)


POST_CHUNKS = tuple((lo, min(lo + 3 * MXU_TILE, D_FF)) for lo in range(0, D_FF, 3 * MXU_TILE))


def _post_kernel(h_ref, modm_ref, ylru_ref, ymla_ref, yna_ref, w1_ref, w2_ref, w3_ref,
                 modf_ref, g_ref, win_ref, wout_ref, o_ref):
    att = jnp.concatenate([_dot_tn(ymla_ref[0, s], w2_ref[...]) + _dot_tn(yna_ref[0, s], w3_ref[...])
                           for s in range(GROUPS)], axis=0)
    y = _dot(ylru_ref[0], w1_ref[...]) + att
    row = pl.program_id(1) * TOK + lax.broadcasted_iota(jnp.int32, (TOK, 1), 0)
    gate = jnp.where(row < CTX, modm_ref[0, 0, 0][2:3], modm_ref[0, 1, 0][2:3])
    h = h_ref[0] + gate * y
    o_ref[0] = _ffn_body(h, modf_ref, g_ref, win_ref, wout_ref, POST_CHUNKS)


def _post_call(h, mod, ylru, ymla, yna, p, g, w_in, w_out, layer):
    return pl.pallas_call(
        _post_kernel,
        grid=(B, NTOK),
        in_specs=[_tok_spec(D), _mod_spec(1), _tok_spec(LRU_W), _tile_t_spec(MLA_H * MLA_V), _tile_t_spec(NA_W),
                  _const_spec((LRU_W, D)), _const_spec((MLA_H * MLA_V, D)), _const_spec((NA_W, D)),
                  _mod_spec(2), _const_spec((1, D)),
                  _layer_spec((D, 2 * D_FF), layer), _layer_spec((D_FF, D), layer)],
        out_specs=_tok_spec(D),
        out_shape=jax.ShapeDtypeStruct((B, TT, D), F32),
        compiler_params=_cparams(("parallel", "arbitrary")),
        name="post",
    )(h, mod, ylru, ymla, yna, p["w_out_lru"], p["w_out_mla"], p["w_out_na"], mod, g, w_in, w_out)


LRU_CHUNK = 256
LRU_PAD = 8
LRU_XROWS = TT + 3 * LRU_PAD
SUB = 8


def _scan_block(a, u, carry, reverse):
    row = lax.broadcasted_iota(jnp.int32, a.shape, 0)
    for s in (1, 2, 4):
        if reverse:
            keep = row < SUB - s
            a_s = jnp.where(keep, pltpu.roll(a, SUB - s, 0), 1.0)
            u_s = jnp.where(keep, pltpu.roll(u, SUB - s, 0), 0.0)
        else:
            keep = row >= s
            a_s = jnp.where(keep, pltpu.roll(a, s, 0), 1.0)
            u_s = jnp.where(keep, pltpu.roll(u, s, 0), 0.0)
        u = a * u_s + u
        a = a * a_s
    h = a * carry + u
    edge = h[0:1] if reverse else h[SUB - 1:SUB]
    return h, jnp.broadcast_to(edge, a.shape)


LRU_SLICE = 64
LRU_NCHUNK = TT // LRU_CHUNK


def _lru_kernel(lx_ref, lg_ref, cw_ref, cb_ref, wg_ref, bg_ref, lam_ref, y_ref, xpad, hf, hb, au0, au1):
    zeros_pad = jnp.zeros((LRU_PAD, LANES), F32)
    xpad[0:LRU_PAD] = zeros_pad
    xpad[LRU_PAD:LRU_PAD + CTX] = lx_ref[0, 0:CTX]
    xpad[LRU_PAD + CTX:2 * LRU_PAD + CTX] = zeros_pad
    xpad[2 * LRU_PAD + CTX:2 * LRU_PAD + TT] = lx_ref[0, CTX:TT]
    xpad[2 * LRU_PAD + TT:3 * LRU_PAD + TT] = zeros_pad

    neg_c_softplus = -LRU_C * jax.nn.softplus(-lam_ref[...])
    n_slices = LRU_CHUNK // LRU_SLICE
    blocks_per_slice = LRU_SLICE // SUB

    def fwd_chunk(i):
        return i

    def bwd_chunk(i):
        return jnp.where(i == 0, 0, LRU_NCHUNK - i)

    def gates(c, d, s, buf):
        base = pl.multiple_of(jnp.where(c == 0, LRU_PAD, 2 * LRU_PAD + c * LRU_CHUNK), SUB) + s * LRU_SLICE
        xc = jnp.broadcast_to(cb_ref[...], (LRU_SLICE, LANES))
        for j in range(CONV_W):
            xc = xc + xpad[pl.ds(base + (j - CONV_LEFT), LRU_SLICE), :] * cw_ref[j:j + 1, :]
        cols = slice(d * 2 * LANES, (d + 1) * 2 * LANES)
        pre = _dot(xc.astype(BF16), wg_ref[0, :, cols]) + bg_ref[0, :, cols]
        r = jax.nn.sigmoid(pre[:, :LANES])
        gate_in = jax.nn.sigmoid(pre[:, LANES:])
        log_a = r * neg_c_softplus[d:d + 1]
        rows = slice(s * LRU_SLICE, (s + 1) * LRU_SLICE)
        buf[2 * d, rows, :] = jnp.exp(log_a)
        th = jnp.tanh(log_a)
        buf[2 * d + 1, rows, :] = jnp.sqrt(-2.0 * th / (1.0 - th)) * (gate_in * xc)

    def scan(c, d, s, buf, carry, dst):
        row0 = pl.multiple_of(c * LRU_CHUNK, SUB)
        for k in range(blocks_per_slice):
            blk = s * blocks_per_slice + k
            if d == 1:
                blk = LRU_CHUNK // SUB - 1 - blk
            rows = slice(blk * SUB, (blk + 1) * SUB)
            h, carry = _scan_block(buf[2 * d, rows, :], buf[2 * d + 1, rows, :], carry, d == 1)
            dst[pl.ds(row0 + blk * SUB, SUB), :] = h
        return carry

    def step(i, cur, nxt, carry):
        cf, cb = carry
        for s in range(n_slices):
            if nxt is not None:
                gates(fwd_chunk(i + 1), 0, s, nxt)
                gates(bwd_chunk(i + 1), 1, s, nxt)
            cf = scan(fwd_chunk(i), 0, s, cur, cf, hf)
            cb = scan(bwd_chunk(i), 1, s, cur, cb, hb)
        return cf, cb

    for s in range(n_slices):
        gates(0, 0, s, au0)
        gates(0, 1, s, au0)

    def pair(j, carry):
        carry = step(2 * j, au0, au1, carry)
        return step(2 * j + 1, au1, au0, carry)

    assert LRU_NCHUNK % 2 == 1
    zero = jnp.zeros((SUB, LANES), F32)
    carry = lax.fori_loop(0, LRU_NCHUNK // 2, pair, (zero, zero))
    step(LRU_NCHUNK - 1, au0, None, carry)

    def finish(c, _):
        rows = pl.ds(pl.multiple_of(c * LRU_CHUNK, LRU_CHUNK), LRU_CHUNK)
        y_ref[0, rows, :] = ((hf[rows, :] + hb[rows, :]) * jax.nn.gelu(lg_ref[0, rows, :])).astype(y_ref.dtype)
        return 0

    lax.fori_loop(0, LRU_NCHUNK, finish, 0)


def _lru_call(lx, lg, p):
    nl = LRU_W // LANES
    seq_spec = pl.BlockSpec((1, TT, LANES), lambda b, j: (b, 0, j))
    return pl.pallas_call(
        _lru_kernel,
        grid=(B, nl),
        in_specs=[seq_spec, seq_spec,
                  pl.BlockSpec((CONV_W, LANES), lambda b, j: (0, j)),
                  pl.BlockSpec((1, LANES), lambda b, j: (0, j)),
                  pl.BlockSpec((1, LANES, 4 * LANES), lambda b, j: (j, 0, 0)),
                  pl.BlockSpec((1, 1, 4 * LANES), lambda b, j: (j, 0, 0)),
                  pl.BlockSpec((2, LANES), lambda b, j: (0, j))],
        out_specs=seq_spec,
        out_shape=jax.ShapeDtypeStruct((B, TT, LRU_W), BF16),
        scratch_shapes=[pltpu.VMEM((LRU_XROWS, LANES), F32), pltpu.VMEM((TT, LANES), F32),
                        pltpu.VMEM((TT, LANES), F32),
                        pltpu.VMEM((4, LRU_CHUNK, LANES), F32), pltpu.VMEM((4, LRU_CHUNK, LANES), F32)],
        compiler_params=_cparams(("parallel", "arbitrary")),
        name="lru",
    )(lx, lg, p["conv_w"], p["conv_b"], p["lru_wg"], p["lru_bg"], p["lru_lam"])


MLA_TQ = 2048
MLA_TK = 256
MLA_NKV = TT // MLA_TK
MLA_UNROLL = 8
MLA_QCOL = 256


def _with_ones(vt):
    return jnp.concatenate([vt, jnp.ones((V_ROWS - vt.shape[0], vt.shape[1]), vt.dtype)], axis=0)


def _softmax_pv(s, vt, m, acc):
    m_new = jnp.maximum(m, jnp.max(s, axis=0, keepdims=True))
    p = jnp.exp2(s - m_new).astype(BF16)
    return m_new, acc * jnp.exp2(m - m_new) + _dot(vt, p)


MLA_NCOL = MLA_TQ // MLA_QCOL
MLA_COLS = [slice(c * MLA_QCOL, (c + 1) * MLA_QCOL) for c in range(MLA_NCOL)]


def _attend(q_cols, q_next_cols, k_ref, vt_ref, cur, nxt):
    tk = MLA_TK
    assert MLA_NKV % MLA_UNROLL == 1 and MLA_UNROLL % 2 == 0

    def keys(i):
        return k_ref[0, pl.ds(pl.multiple_of(i * tk, tk), tk), :]

    def values(i):
        return _with_ones(vt_ref[0, i])

    def group(j, carry):
        ms, accs = list(carry[0]), list(carry[1])
        a, b = cur, nxt
        for u in range(MLA_UNROLL):
            i = MLA_UNROLL * j + u
            k_next, v_cur = keys(i + 1), values(i)
            for c, sl in enumerate(MLA_COLS):
                b[:, sl] = _dot(k_next, q_cols[c])
                ms[c], accs[c] = _softmax_pv(a[:, sl], v_cur, ms[c], accs[c])
            a, b = b, a
        return tuple(ms), tuple(accs)

    init = (tuple(jnp.full((1, MLA_QCOL), NEG, F32) for _ in MLA_COLS),
            tuple(jnp.zeros((V_ROWS, MLA_QCOL), F32) for _ in MLA_COLS))
    ms, accs = lax.fori_loop(0, MLA_NKV // MLA_UNROLL, group, init)
    k_first, v_last = k_ref[0, 0:tk, :], values(MLA_NKV - 1)
    outs = []
    for c, sl in enumerate(MLA_COLS):
        nxt[:, sl] = _dot(k_first, q_next_cols[c])
        _, acc = _softmax_pv(cur[:, sl], v_last, ms[c], accs[c])
        outs.append(acc[:MLA_V] / acc[MLA_V:MLA_V + 1])
    return jnp.concatenate(outs, axis=1)


def _mla_kernel(q_ref, k_ref, vt_ref, o_ref, s_a, s_b):
    sc = _dot(k_ref[0, 0:CTX, :], q_ref[0, 0])
    _, acc = _softmax_pv(sc, _with_ones(vt_ref[0, 0]), jnp.full((1, CTX), NEG, F32),
                         jnp.zeros((V_ROWS, CTX), F32))
    o_ref[0, 0] = (acc[:MLA_V] / acc[MLA_V:MLA_V + 1]).astype(o_ref.dtype)

    n_tiles = SEQ // MLA_TQ
    assert n_tiles % 2 == 0 and MLA_QCOL == TM

    def q_tile(t):
        t = jnp.minimum(t, n_tiles - 1)
        return [q_ref[0, 1 + t * MLA_NCOL + c] for c in range(MLA_NCOL)]

    def store(t, o):
        for g in range(MLA_NCOL):
            o_ref[0, 1 + t * MLA_NCOL + g] = o[:, g * TM:(g + 1) * TM].astype(o_ref.dtype)

    q0 = q_tile(0)
    for c, sl in enumerate(MLA_COLS):
        s_a[:, sl] = _dot(k_ref[0, 0:MLA_TK, :], q0[c])

    def tile_pair(tt, _):
        t = 2 * tt
        store(t, _attend(q_tile(t), q_tile(t + 1), k_ref, vt_ref, s_a, s_b))
        store(t + 1, _attend(q_tile(t + 1), q_tile(t + 2), k_ref, vt_ref, s_b, s_a))
        return 0

    lax.fori_loop(0, n_tiles // 2, tile_pair, 0)


def _mla_call(qm, km, vt):
    assert MLA_TK == TM
    head_t_spec = pl.BlockSpec((1, NT, MLA_V, TM), lambda b, h: (b, 0, h, 0))
    return pl.pallas_call(
        _mla_kernel,
        grid=(B, MLA_H),
        in_specs=[pl.BlockSpec((1, NT, HEAD_SLOT, TM), lambda b, h: (b, 0, h, 0)),
                  pl.BlockSpec((1, TT, HEAD_SLOT), lambda b, h: (b, 0, h)),
                  head_t_spec],
        out_specs=head_t_spec,
        out_shape=jax.ShapeDtypeStruct((B, NT, MLA_H * MLA_V, TM), BF16),
        scratch_shapes=[pltpu.VMEM((MLA_TK, MLA_TQ), F32), pltpu.VMEM((MLA_TK, MLA_TQ), F32)],
        compiler_params=_cparams(("parallel", "arbitrary")),
        name="mla_attn",
    )(qm, km, vt)


def _na_kernel(q_ref, k_ref, vt_ref, tab_ref, o_ref):
    j = pl.program_id(1)
    q_all = q_ref[0, 0]
    row_head = jnp.right_shift(lax.broadcasted_iota(jnp.int32, q_all.shape, 0), 6)
    q_heads = [jnp.where(row_head == h, q_all, jnp.zeros_like(q_all)) for h in range(NA_H)]

    def values(g, h):
        return vt_ref[0, g, h * NA_DH:(h + 1) * NA_DH, :]

    def finish(h, s, vt):
        m = jnp.max(s, axis=0, keepdims=True)
        acc = _dot(_with_ones(vt), jnp.exp(s - m).astype(BF16))
        o_ref[0, 0, h * NA_DH:(h + 1) * NA_DH, :] = (acc[:NA_DH] / acc[NA_DH:NA_DH + 1]).astype(o_ref.dtype)

    @pl.when(j == 0)
    def _():
        for h in range(NA_H):
            finish(h, _dot(k_ref[0, 0], q_heads[h]), values(0, h))

    @pl.when(j > 0)
    def _():
        jj = j - 1
        g0 = jnp.clip(jj - 1, 0, NA_NGROUPS - NA_BAND_G)
        variant = jnp.where(jj == 0, 0, jnp.where(jj == NA_NGROUPS - 1, 2, 1))
        band = k_ref[0, pl.ds(1 + g0, NA_BAND_G)].reshape(NA_BAND_G * TM, NA_W)
        keys = jnp.concatenate([k_ref[0, 0], band], axis=0)
        scores, probs = {}, {}
        for t in range(NA_H + 2):
            if t < NA_H:
                scores[t] = _dot(keys, q_heads[t])
            h = t - 1
            if 0 <= h < NA_H:
                s = scores.pop(h)
                s = jnp.concatenate([s[:TM], s[TM:] + tab_ref[variant, h]], axis=0)
                probs[h] = jnp.exp(s - jnp.max(s, axis=0, keepdims=True)).astype(BF16)
            h = t - 2
            if 0 <= h < NA_H:
                vt = jnp.concatenate([values(0, h)] + [values(1 + g0 + g, h) for g in range(NA_BAND_G)], axis=1)
                acc = _dot(_with_ones(vt), probs.pop(h))
                o_ref[0, 0, h * NA_DH:(h + 1) * NA_DH, :] = (
                    acc[:NA_DH] / acc[NA_DH:NA_DH + 1]).astype(o_ref.dtype)


def _na_call(q, k, vt, tab, layer):
    whole = pl.BlockSpec((1, NT, TM, NA_W), lambda b, j: (b, 0, 0, 0))
    group_t = pl.BlockSpec((1, 1, NA_W, TM), lambda b, j: (b, j, 0, 0))
    return pl.pallas_call(
        _na_kernel,
        grid=(B, NT),
        in_specs=[group_t, whole, whole, _layer_spec((3, NA_H, NA_BAND_G * TM, TM), layer)],
        out_specs=group_t,
        out_shape=jax.ShapeDtypeStruct((B, NT, NA_W, TM), BF16),
        compiler_params=_cparams(("parallel", "arbitrary")),
        name="na_attn",
    )(q, k, vt, tab)


def _rope_tables():
    t = jnp.arange(SEQ)
    pos = jnp.stack([t // GRID_W, t % GRID_W], axis=-1).astype(F32)
    half = MLA_ROPE // 2
    inv = ROPE_BASE ** (-jnp.arange(0, half, 2, dtype=F32) / half)
    ang = pos[:, :, None] * inv
    cos, sin = jnp.cos(ang), jnp.sin(ang)
    ones = jnp.ones((SEQ, MLA_NOPE), F32)
    c = jnp.concatenate([ones, cos[:, 0], cos[:, 0], cos[:, 1], cos[:, 1], ones[:, :32]], axis=-1)
    s = jnp.concatenate([0 * ones, -sin[:, 0], sin[:, 0], -sin[:, 1], sin[:, 1], 0 * ones[:, :32]], axis=-1)
    ctx_c = jnp.ones((CTX, LANES), F32)
    ctx_s = jnp.zeros((CTX, LANES), F32)
    return jnp.concatenate([ctx_c, c]), jnp.concatenate([ctx_s, s])


ROPE_PARTNER = np.concatenate([np.arange(8, 16), np.arange(0, 8), np.arange(24, 32), np.arange(16, 24)])


def _group_ones(width, group):
    g = np.arange(width) // group
    return jnp.asarray(g[:, None] == g[None, :], BF16)


def _na_window_pattern():
    n_band = NA_BAND_G * NA_QROWS
    i = np.arange(n_band)[:, None]
    a = np.arange(NA_QROWS)[None, :]

    def pattern(jj):
        g0 = min(max(jj - 1, 0), NA_NGROUPS - NA_BAND_G)
        krow = NA_QROWS * g0 + i
        qrow = NA_QROWS * jj + a
        r0q = np.clip(qrow - NA_WR // 2, 0, GRID_H - NA_WR)
        return (krow >= r0q) & (krow < r0q + NA_WR), krow - qrow + NA_WR - 1

    interior = [pattern(jj) for jj in range(1, NA_NGROUPS - 1)]
    assert all((v == interior[0][0]).all() and (r[v] == interior[0][1][v]).all() for v, r in interior)
    pats = [pattern(0), interior[0], pattern(NA_NGROUPS - 1)]
    valid = np.stack([p[0] for p in pats])
    row_off = np.stack([p[1] for p in pats])
    assert (row_off[valid] >= 0).all() and (row_off[valid] <= 2 * NA_WR - 2).all()
    return valid, row_off


def _na_tables(rpb):
    kc = jnp.arange(GRID_W)[:, None]
    qc = jnp.arange(GRID_W)[None, :]
    win = jnp.clip(qc - NA_WC // 2, 0, GRID_W - NA_WC)
    ok_col = (kc >= win) & (kc < win + NA_WC)
    col_off = kc - qc + NA_WC - 1
    pick_col = ((col_off[None] == jnp.arange(2 * NA_WC - 1)[:, None, None]) & ok_col[None]).astype(F32)
    valid, row_off = _na_window_pattern()
    pick_row = ((row_off[..., None] == jnp.arange(2 * NA_WR - 1)) & valid[..., None]).astype(F32)
    tab = jnp.einsum("vian,dhnj,jkq->dvhikaq", pick_row, rpb, pick_col, precision=lax.Precision.HIGHEST)
    keep = valid[:, None, :, None, :, None] & ok_col[None, None, None, :, None, :]
    tab = jnp.where(keep, tab, NEG)
    return tab.reshape(DEPTH, 3, NA_H, NA_BAND_G * TM, TM)


def _block_diag(w):
    w = w.reshape(DEPTH, 2, LRU_W // LANES, 2, LRU_BW, LRU_BW)
    z = jnp.zeros_like(w[:, :, :, 0])
    top = jnp.concatenate([w[:, :, :, 0], z], axis=-1)
    bot = jnp.concatenate([z, w[:, :, :, 1]], axis=-1)
    return jnp.concatenate([top, bot], axis=-2)


def _prepare(w_in, w_out, lru_conv_b, lru_w_a, lru_b_a, lru_w_x, lru_b_x, mla_w_uq, mla_w_ukv,
             mla_q_gain, mla_k_gain, na_q_gain, na_k_gain):
    z = lambda n: jnp.zeros((DEPTH, D, n), F32)
    kr = w_in[..., 1152:1184]
    w_in_ext = jnp.concatenate(
        [w_in[..., :1152], z(MLA_NOPE), kr, kr[..., ROPE_PARTNER], w_in[..., 1184:]], axis=-1).astype(BF16)
    w_uq = mla_w_uq.reshape(DEPTH, MLA_QR, MLA_H, MLA_QK)
    slot_pad = ((0, 0), (0, 0), (0, 0), (0, HEAD_SLOT - MLA_QK))
    zq = jnp.zeros((DEPTH, MLA_QR, MLA_H, MLA_NOPE), F32)
    w_uq_partner = jnp.concatenate([zq, w_uq[..., MLA_NOPE + ROPE_PARTNER]], axis=-1)
    w_uq = jnp.concatenate([jnp.pad(w_uq, slot_pad).reshape(DEPTH, MLA_QR, -1),
                            jnp.pad(w_uq_partner, slot_pad).reshape(DEPTH, MLA_QR, -1)], axis=-1)
    w_ukv = mla_w_ukv.reshape(DEPTH, MLA_KVR, MLA_H, MLA_NOPE + MLA_V)
    w_uk = jnp.pad(w_ukv[..., :MLA_NOPE], ((0, 0), (0, 0), (0, 0), (0, HEAD_SLOT - MLA_NOPE)))
    w_ukv_ext = jnp.concatenate([w_uk.reshape(DEPTH, MLA_KVR, -1),
                                 w_ukv[..., MLA_NOPE:].reshape(DEPTH, MLA_KVR, -1)], axis=-1)

    def pad_gain(g):
        partner = jnp.concatenate([jnp.zeros((DEPTH, MLA_NOPE), F32), g[:, MLA_NOPE + ROPE_PARTNER]], axis=-1)
        both = jnp.stack([g, partner], axis=1)
        return jnp.pad(both, ((0, 0), (0, 0), (0, HEAD_SLOT - MLA_QK)))
    tile_gain = lambda g: jnp.tile(g, (1, NA_H)).reshape(DEPTH, 1, NA_W)
    wa, wx = _block_diag(lru_w_a), _block_diag(lru_w_x)
    wg = jnp.concatenate([wa[:, 0], wx[:, 0], wa[:, 1], wx[:, 1]], axis=-1).astype(BF16)
    ba = lru_b_a.reshape(DEPTH, 2, LRU_W // LANES, 1, LANES)
    bx = lru_b_x.reshape(DEPTH, 2, LRU_W // LANES, 1, LANES)
    bg = jnp.concatenate([ba[:, 0], bx[:, 0], ba[:, 1], bx[:, 1]], axis=-1)
    w_out = w_out.astype(BF16)
    return dict(
        w_in=w_in_ext, w_uq=w_uq.astype(BF16), w_ukv=w_ukv_ext.astype(BF16),
        mq_gain=pad_gain(mla_q_gain), mk_gain=pad_gain(mla_k_gain),
        nq_gain=tile_gain(na_q_gain), nk_gain=tile_gain(na_k_gain),
        lru_wg=wg, lru_bg=bg, conv_b=lru_conv_b.reshape(DEPTH, 1, LRU_W),
        w_out_lru=w_out[:, :LRU_W], w_out_mla=w_out[:, LRU_W:LRU_W + MLA_H * MLA_V],
        w_out_na=w_out[:, LRU_W + MLA_H * MLA_V:],
    )


def kernel(x, c, ctx, c_ctx, w_mod, b_mod, norm_ffn1, ffn1_w_in, ffn1_w_out, norm_mix, w_in, w_out, lru_conv_w, lru_conv_b, lru_w_a, lru_b_a, lru_w_x, lru_b_x, lru_lambda, mla_q_norm, mla_w_uq, mla_kv_norm, mla_w_ukv, mla_q_gain, mla_k_gain, na_q_gain, na_k_gain, na_rpb, norm_ffn2, ffn2_w_in, ffn2_w_out):
    cc = jnp.concatenate([c, c_ctx[None], jnp.zeros((8 - B - 1, D), F32)], axis=0)
    mod = _mod_call(cc, w_mod, b_mod)
    mod = mod.reshape(DEPTH, 8, 3, 3, D)
    mod = jnp.stack([jnp.broadcast_to(mod[:, B:B + 1], (DEPTH, B, 3, 3, D)), mod[:, :B]], axis=2)

    prm = _prepare(w_in, w_out, lru_conv_b, lru_w_a, lru_b_a, lru_w_x, lru_b_x, mla_w_uq, mla_w_ukv,
                   mla_q_gain, mla_k_gain, na_q_gain, na_k_gain)
    prm.update(norm_mix=norm_mix.reshape(DEPTH, 1, D), q_norm=mla_q_norm.reshape(DEPTH, 1, MLA_QR),
               kv_norm=mla_kv_norm.reshape(DEPTH, 1, MLA_KVR), conv_w=lru_conv_w, lru_lam=lru_lambda)
    f1_in, f1_out = ffn1_w_in.astype(BF16), ffn1_w_out.astype(BF16)
    f2_in, f2_out = ffn2_w_in.astype(BF16), ffn2_w_out.astype(BF16)
    g1, g2 = norm_ffn1.reshape(DEPTH, 1, D), norm_ffn2.reshape(DEPTH, 1, D)
    rope_tabs = _rope_tables() + (_group_ones(MXU_TILE, HEAD_SLOT), _group_ones(MXU_TILE, NA_DH))
    na_tabs = _na_tables(na_rpb)

    h = jnp.concatenate([ctx, x], axis=1)
    for l in range(DEPTH):
        p = {k: v[l] for k, v in prm.items()}
        h = _ffn_call(h, mod[l], g1[l], f1_in, f1_out, l, 0, "ffn1")
        lx, lg, qm, km, vm, nq, nk, nv = _mixin_call(h, mod[l], p, rope_tabs)
        ylru = _lru_call(lx, lg, p)
        ymla = _mla_call(qm, km, vm)
        yna = _na_call(nq, nk.reshape(B, NT, TM, NA_W), nv, na_tabs, l)
        h = _post_call(h, mod[l], ylru, ymla, yna, p, g2[l], f2_in, f2_out, l)
    return h[:, CTX:]
```

```python
import functools
import math

import jax
import jax.numpy as jnp
import numpy as np
from jax import lax
from jax.experimental import pallas as pl
from jax.experimental.pallas import tpu as pltpu

F32 = jnp.float32
BF16 = jnp.bfloat16

D = 1024
B = 4
SEQ = 8192
DEPTH = 4
GRID_W = 64
GRID_H = SEQ // GRID_W
CTX = 256
TT = CTX + SEQ
D_FF = 2816
EPS = 1e-6

LRU_W = 384
LRU_BW = 64
LRU_C = 8.0
CONV_W = 4
CONV_LEFT = 2

MLA_H = 6
MLA_QR = 256
MLA_KVR = 128
MLA_NOPE = 64
MLA_ROPE = 32
MLA_QK = MLA_NOPE + MLA_ROPE
MLA_V = 64
ROPE_BASE = 10000.0

NA_H = 4
NA_DH = 64
NA_W = NA_H * NA_DH
NA_WR = 8
NA_WC = 16

LANES = 128
HEAD_SLOT = LANES
IN_EXT = 2048
TM = 256
NT = TT // TM
V_ROWS = 80
NEG = -1e30
VMEM_LIMIT = 56 * 1024 * 1024

NA_QROWS = 4
NA_GROUP = NA_QROWS * GRID_W
NA_NGROUPS = SEQ // NA_GROUP
NA_BAND_G = 3

MLA_QSCALE = MLA_QK ** -0.5 * math.log2(math.e)
NA_QSCALE = NA_DH ** -0.5


def _cparams(sem):
    return pltpu.CompilerParams(dimension_semantics=sem, vmem_limit_bytes=VMEM_LIMIT)


def _dot(a, b):
    return jnp.dot(a, b, preferred_element_type=F32)


def _dot_nt(a, b):
    return lax.dot_general(a, b, (((1,), (1,)), ((), ())), preferred_element_type=F32)


def _rms(x, n=None):
    n = x.shape[-1] if n is None else n
    return x * lax.rsqrt(jnp.sum(x * x, axis=-1, keepdims=True) * (1.0 / n) + EPS)


def _mod_kernel(c_ref, w_ref, b_ref, o_ref):
    s = jax.nn.silu(c_ref[...]).astype(BF16)
    o_ref[0] = _dot(s, w_ref[0].astype(BF16)) + b_ref[0]


def _mod_call(cc, w_mod, b_mod):
    n = w_mod.shape[-1]
    tn = n // 4
    return pl.pallas_call(
        _mod_kernel,
        grid=(DEPTH, n // tn),
        in_specs=[
            pl.BlockSpec((8, D), lambda l, j: (0, 0)),
            pl.BlockSpec((1, D, tn), lambda l, j: (l, 0, j)),
            pl.BlockSpec((1, 1, tn), lambda l, j: (l, 0, j)),
        ],
        out_specs=pl.BlockSpec((1, 8, tn), lambda l, j: (l, 0, j)),
        out_shape=jax.ShapeDtypeStruct((DEPTH, 8, n), F32),
        compiler_params=_cparams(("arbitrary", "arbitrary")),
        name="mod",
    )(cc, w_mod, b_mod.reshape(DEPTH, 1, n))


TOK = 768
GROUPS = TOK // TM
NTOK = TT // TOK


def _norm_mod(x, g, mod_ref):
    row = pl.program_id(1) * TOK + lax.broadcasted_iota(jnp.int32, (x.shape[0], 1), 0)
    is_ctx = row < CTX
    mod_c, mod_l = mod_ref[0, 0, 0], mod_ref[0, 1, 0]
    pick = lambda r: jnp.where(is_ctx, mod_c[r:r + 1], mod_l[r:r + 1])
    xn = _rms(x) * g
    return xn * (1.0 + pick(1)) + pick(0), pick(2)


MXU_TILE = 256
FFN_SPLIT = (D_FF // MXU_TILE + 1) // 2 * MXU_TILE
FFN_CHUNKS = ((0, FFN_SPLIT), (FFN_SPLIT, D_FF))


def _ffn_body(x, mod_ref, g_ref, win_ref, wout_ref, chunks):
    xm, gate_row = _norm_mod(x, g_ref[...], mod_ref)
    xb = xm.astype(BF16)
    acc = jnp.zeros((x.shape[0], D), F32)
    for lo, hi in chunks:
        gate = _dot(xb, win_ref[:, lo:hi])
        up = _dot(xb, win_ref[:, D_FF + lo:D_FF + hi])
        a = (jax.nn.silu(gate) * up).astype(BF16)
        acc = acc + _dot(a, wout_ref[lo:hi, :])
    return x + 0.5 * gate_row * acc


def _ffn_kernel(x_ref, mod_ref, g_ref, win_ref, wout_ref, o_ref):
    o_ref[0] = _ffn_body(x_ref[0], mod_ref, g_ref, win_ref, wout_ref, FFN_CHUNKS)


def _mod_spec(group):
    return pl.BlockSpec((1, 2, 1, 3, D), lambda b, i: (b, 0, group, 0, 0))


def _tok_spec(width):
    return pl.BlockSpec((1, TOK, width), lambda b, i: (b, i, 0))


def _const_spec(shape):
    return pl.BlockSpec(shape, lambda b, i: (0,) * len(shape), pipeline_mode=pl.Buffered(1))


def _layer_spec(shape, layer):
    return pl.BlockSpec((None,) + shape, lambda b, i: (layer,) + (0,) * len(shape),
                        pipeline_mode=pl.Buffered(1))


def _ffn_call(h, mod, g, w_in, w_out, layer, group, name):
    return pl.pallas_call(
        _ffn_kernel,
        grid=(B, NTOK),
        in_specs=[_tok_spec(D), _mod_spec(group), _const_spec((1, D)),
                  _layer_spec((D, 2 * D_FF), layer), _layer_spec((D_FF, D), layer)],
        out_specs=_tok_spec(D),
        out_shape=jax.ShapeDtypeStruct((B, TT, D), F32),
        compiler_params=_cparams(("parallel", "arbitrary")),
        name=name,
    )(h, mod, g, w_in, w_out)


def _group_ssq(x, ones_ref):
    sq = x * x
    hi = sq.astype(BF16)
    lo = (sq - hi.astype(F32)).astype(BF16)
    ones = ones_ref[...]
    tiles = [slice(c, c + MXU_TILE) for c in range(0, x.shape[1], MXU_TILE)]
    return jnp.concatenate([_dot(hi[:, t], ones) + _dot(lo[:, t], ones) for t in tiles], axis=1)


def _mixin_kernel(x_ref, mod_ref, g_ref, win_ref, qn_ref, wuq_ref, kvn_ref, wukv_ref,
                  mqg_ref, mkg_ref, nqg_ref, nkg_ref, rc_ref, rs_ref, ones_h_ref, ones_n_ref,
                  lx_ref, lg_ref, qm_ref, km_ref, vm_ref, nq_ref, nk_ref, nv_ref):
    x = x_ref[0]
    xb = _norm_mod(x, g_ref[...], mod_ref)[0].astype(BF16)
    y = _dot(xb, win_ref[...])
    lx_ref[0] = y[:, 0:LRU_W]
    lg_ref[0] = y[:, LRU_W:2 * LRU_W]
    cq = y[:, 768:1024]
    ckv = y[:, 1024:1152]
    kr_slot = y[:, 1152:1280]
    rc, rs = rc_ref[...], rs_ref[...]
    n_slot = MLA_H * HEAD_SLOT

    q2 = _dot((_rms(cq) * qn_ref[...]).astype(BF16), wuq_ref[...])
    q_raw = q2[:, :n_slot]
    inv_q = lax.rsqrt(_group_ssq(q_raw, ones_h_ref) * (1.0 / MLA_QK) + EPS)
    cos_q = rc * (mqg_ref[0:1] * MLA_QSCALE)
    sin_q = rs * (mqg_ref[1:2] * MLA_QSCALE)

    kv = _dot((_rms(ckv) * kvn_ref[...]).astype(BF16), wukv_ref[...])
    lane = lax.broadcasted_iota(jnp.int32, kr_slot.shape, 1)
    kr = jnp.where(lane < MLA_QK, kr_slot, 0.0)
    kr_partner = pltpu.roll(kr_slot, LANES - MLA_ROPE, 1)
    k_raw = kv[:, :n_slot] + jnp.concatenate([kr] * MLA_H, axis=1)
    inv_k = lax.rsqrt(_group_ssq(k_raw, ones_h_ref) * (1.0 / MLA_QK) + EPS)
    cos_k = rc * mkg_ref[0:1]
    sin_k = kr_partner * (rs * mkg_ref[1:2])
    q_heads = []
    for h in range(MLA_H):
        sl = slice(h * HEAD_SLOT, (h + 1) * HEAD_SLOT)
        qh = q_raw[:, sl] * cos_q + q2[:, n_slot + h * HEAD_SLOT:n_slot + (h + 1) * HEAD_SLOT] * sin_q
        q_heads.append(qh * inv_q[:, sl])
        km_ref[0, :, sl] = ((k_raw[:, sl] * cos_k + sin_k) * inv_k[:, sl]).astype(BF16)
    q_all = jnp.concatenate(q_heads, axis=1)
    for s in range(GROUPS):
        qm_ref[0, s] = q_all[s * TM:(s + 1) * TM].T.astype(BF16)
        vm_ref[0, s] = kv[s * TM:(s + 1) * TM, n_slot:].T.astype(BF16)

    nq = y[:, 1280:1536]
    nq = nq * lax.rsqrt(_group_ssq(nq, ones_n_ref) * (1.0 / NA_DH) + EPS) * (nqg_ref[...] * NA_QSCALE)
    nk = y[:, 1536:1792]
    nk_ref[0] = (nk * lax.rsqrt(_group_ssq(nk, ones_n_ref) * (1.0 / NA_DH) + EPS) * nkg_ref[...]).astype(BF16)
    for s in range(GROUPS):
        nq_ref[0, s] = nq[s * TM:(s + 1) * TM].T.astype(BF16)
        nv_ref[0, s] = y[s * TM:(s + 1) * TM, 1792:2048].T.astype(BF16)


def _tile_t_spec(rows):
    return pl.BlockSpec((1, GROUPS, rows, TM), lambda b, i: (b, i, 0, 0))


def _mixin_call(h, mod, p, tabs):
    n_slot = MLA_H * HEAD_SLOT
    tok = lambda w, dt: (_tok_spec(w), jax.ShapeDtypeStruct((B, TT, w), dt))
    tile_t = lambda rows: (_tile_t_spec(rows), jax.ShapeDtypeStruct((B, NT, rows, TM), BF16))
    outs = [tok(LRU_W, F32), tok(LRU_W, F32), tile_t(n_slot), tok(n_slot, BF16), tile_t(MLA_H * MLA_V),
            tile_t(NA_W), tok(NA_W, BF16), tile_t(NA_W)]
    out_specs, out_shape = [o[0] for o in outs], [o[1] for o in outs]
    rope_spec = pl.BlockSpec((TOK, LANES), lambda b, i: (i, 0))
    return pl.pallas_call(
        _mixin_kernel,
        grid=(B, NTOK),
        in_specs=[_tok_spec(D), _mod_spec(1), _const_spec((1, D)), _const_spec((D, IN_EXT)),
                  _const_spec((1, MLA_QR)), _const_spec((MLA_QR, 2 * n_slot)),
                  _const_spec((1, MLA_KVR)), _const_spec((MLA_KVR, n_slot + MLA_H * MLA_V)),
                  _const_spec((2, HEAD_SLOT)), _const_spec((2, HEAD_SLOT)),
                  _const_spec((1, NA_W)), _const_spec((1, NA_W)),
                  rope_spec, rope_spec, _const_spec((MXU_TILE, MXU_TILE)), _const_spec((MXU_TILE, MXU_TILE))],
        out_specs=out_specs,
        out_shape=out_shape,
        compiler_params=_cparams(("parallel", "arbitrary")),
        name="mix_in",
    )(h, mod, p["norm_mix"], p["w_in"], p["q_norm"], p["w_uq"], p["kv_norm"], p["w_ukv"],
      p["mq_gain"], p["mk_gain"], p["nq_gain"], p["nk_gain"], *tabs)


def _dot_tn(a, b):
    return lax.dot_general(a, b, (((0,), (0,)), ((), ())), preferred_element_type=F32)


POST_CHUNKS = tuple((lo, min(lo + 3 * MXU_TILE, D_FF)) for lo in range(0, D_FF, 3 * MXU_TILE))


def _post_kernel(h_ref, modm_ref, ylru_ref, ymla_ref, yna_ref, w1_ref, w2_ref, w3_ref,
                 modf_ref, g_ref, win_ref, wout_ref, o_ref):
    att = jnp.concatenate([_dot_tn(ymla_ref[0, s], w2_ref[...]) + _dot_tn(yna_ref[0, s], w3_ref[...])
                           for s in range(GROUPS)], axis=0)
    y = _dot(ylru_ref[0], w1_ref[...]) + att
    row = pl.program_id(1) * TOK + lax.broadcasted_iota(jnp.int32, (TOK, 1), 0)
    gate = jnp.where(row < CTX, modm_ref[0, 0, 0][2:3], modm_ref[0, 1, 0][2:3])
    h = h_ref[0] + gate * y
    o_ref[0] = _ffn_body(h, modf_ref, g_ref, win_ref, wout_ref, POST_CHUNKS)


def _post_call(h, mod, ylru, ymla, yna, p, g, w_in, w_out, layer):
    return pl.pallas_call(
        _post_kernel,
        grid=(B, NTOK),
        in_specs=[_tok_spec(D), _mod_spec(1), _tok_spec(LRU_W), _tile_t_spec(MLA_H * MLA_V), _tile_t_spec(NA_W),
                  _const_spec((LRU_W, D)), _const_spec((MLA_H * MLA_V, D)), _const_spec((NA_W, D)),
                  _mod_spec(2), _const_spec((1, D)),
                  _layer_spec((D, 2 * D_FF), layer), _layer_spec((D_FF, D), layer)],
        out_specs=_tok_spec(D),
        out_shape=jax.ShapeDtypeStruct((B, TT, D), F32),
        compiler_params=_cparams(("parallel", "arbitrary")),
        name="post",
    )(h, mod, ylru, ymla, yna, p["w_out_lru"], p["w_out_mla"], p["w_out_na"], mod, g, w_in, w_out)


LRU_CHUNK = 256
LRU_PAD = 8
LRU_XROWS = TT + 3 * LRU_PAD
SUB = 8


def _scan_block(a, u, carry, reverse):
    row = lax.broadcasted_iota(jnp.int32, a.shape, 0)
    for s in (1, 2, 4):
        if reverse:
            keep = row < SUB - s
            a_s = jnp.where(keep, pltpu.roll(a, SUB - s, 0), 1.0)
            u_s = jnp.where(keep, pltpu.roll(u, SUB - s, 0), 0.0)
        else:
            keep = row >= s
            a_s = jnp.where(keep, pltpu.roll(a, s, 0), 1.0)
            u_s = jnp.where(keep, pltpu.roll(u, s, 0), 0.0)
        u = a * u_s + u
        a = a * a_s
    h = a * carry + u
    edge = h[0:1] if reverse else h[SUB - 1:SUB]
    return h, jnp.broadcast_to(edge, a.shape)


LRU_SLICE = 64
LRU_NCHUNK = TT // LRU_CHUNK


def _lru_kernel(lx_ref, lg_ref, cw_ref, cb_ref, wg_ref, bg_ref, lam_ref, y_ref, xpad, hf, hb, au0, au1):
    zeros_pad = jnp.zeros((LRU_PAD, LANES), F32)
    xpad[0:LRU_PAD] = zeros_pad
    xpad[LRU_PAD:LRU_PAD + CTX] = lx_ref[0, 0:CTX]
    xpad[LRU_PAD + CTX:2 * LRU_PAD + CTX] = zeros_pad
    xpad[2 * LRU_PAD + CTX:2 * LRU_PAD + TT] = lx_ref[0, CTX:TT]
    xpad[2 * LRU_PAD + TT:3 * LRU_PAD + TT] = zeros_pad

    neg_c_softplus = -LRU_C * jax.nn.softplus(-lam_ref[...])
    n_slices = LRU_CHUNK // LRU_SLICE
    blocks_per_slice = LRU_SLICE // SUB

    def fwd_chunk(i):
        return i

    def bwd_chunk(i):
        return jnp.where(i == 0, 0, LRU_NCHUNK - i)

    def gates(c, d, s, buf):
        base = pl.multiple_of(jnp.where(c == 0, LRU_PAD, 2 * LRU_PAD + c * LRU_CHUNK), SUB) + s * LRU_SLICE
        xc = jnp.broadcast_to(cb_ref[...], (LRU_SLICE, LANES))
        for j in range(CONV_W):
            xc = xc + xpad[pl.ds(base + (j - CONV_LEFT), LRU_SLICE), :] * cw_ref[j:j + 1, :]
        cols = slice(d * 2 * LANES, (d + 1) * 2 * LANES)
        pre = _dot(xc.astype(BF16), wg_ref[0, :, cols]) + bg_ref[0, :, cols]
        r = jax.nn.sigmoid(pre[:, :LANES])
        gate_in = jax.nn.sigmoid(pre[:, LANES:])
        log_a = r * neg_c_softplus[d:d + 1]
        rows = slice(s * LRU_SLICE, (s + 1) * LRU_SLICE)
        buf[2 * d, rows, :] = jnp.exp(log_a)
        th = jnp.tanh(log_a)
        buf[2 * d + 1, rows, :] = jnp.sqrt(-2.0 * th / (1.0 - th)) * (gate_in * xc)

    def scan(c, d, s, buf, carry, dst):
        row0 = pl.multiple_of(c * LRU_CHUNK, SUB)
        for k in range(blocks_per_slice):
            blk = s * blocks_per_slice + k
            if d == 1:
                blk = LRU_CHUNK // SUB - 1 - blk
            rows = slice(blk * SUB, (blk + 1) * SUB)
            h, carry = _scan_block(buf[2 * d, rows, :], buf[2 * d + 1, rows, :], carry, d == 1)
            dst[pl.ds(row0 + blk * SUB, SUB), :] = h
        return carry

    def step(i, cur, nxt, carry):
        cf, cb = carry
        for s in range(n_slices):
            if nxt is not None:
                gates(fwd_chunk(i + 1), 0, s, nxt)
                gates(bwd_chunk(i + 1), 1, s, nxt)
            cf = scan(fwd_chunk(i), 0, s, cur, cf, hf)
            cb = scan(bwd_chunk(i), 1, s, cur, cb, hb)
        return cf, cb

    for s in range(n_slices):
        gates(0, 0, s, au0)
        gates(0, 1, s, au0)

    def pair(j, carry):
        carry = step(2 * j, au0, au1, carry)
        return step(2 * j + 1, au1, au0, carry)

    assert LRU_NCHUNK % 2 == 1
    zero = jnp.zeros((SUB, LANES), F32)
    carry = lax.fori_loop(0, LRU_NCHUNK // 2, pair, (zero, zero))
    step(LRU_NCHUNK - 1, au0, None, carry)

    def finish(c, _):
        rows = pl.ds(pl.multiple_of(c * LRU_CHUNK, LRU_CHUNK), LRU_CHUNK)
        y_ref[0, rows, :] = ((hf[rows, :] + hb[rows, :]) * jax.nn.gelu(lg_ref[0, rows, :])).astype(y_ref.dtype)
        return 0

    lax.fori_loop(0, LRU_NCHUNK, finish, 0)


def _lru_call(lx, lg, p):
    nl = LRU_W // LANES
    seq_spec = pl.BlockSpec((1, TT, LANES), lambda b, j: (b, 0, j))
    return pl.pallas_call(
        _lru_kernel,
        grid=(B, nl),
        in_specs=[seq_spec, seq_spec,
                  pl.BlockSpec((CONV_W, LANES), lambda b, j: (0, j)),
                  pl.BlockSpec((1, LANES), lambda b, j: (0, j)),
                  pl.BlockSpec((1, LANES, 4 * LANES), lambda b, j: (j, 0, 0)),
                  pl.BlockSpec((1, 1, 4 * LANES), lambda b, j: (j, 0, 0)),
                  pl.BlockSpec((2, LANES), lambda b, j: (0, j))],
        out_specs=seq_spec,
        out_shape=jax.ShapeDtypeStruct((B, TT, LRU_W), BF16),
        scratch_shapes=[pltpu.VMEM((LRU_XROWS, LANES), F32), pltpu.VMEM((TT, LANES), F32),
                        pltpu.VMEM((TT, LANES), F32),
                        pltpu.VMEM((4, LRU_CHUNK, LANES), F32), pltpu.VMEM((4, LRU_CHUNK, LANES), F32)],
        compiler_params=_cparams(("parallel", "arbitrary")),
        name="lru",
    )(lx, lg, p["conv_w"], p["conv_b"], p["lru_wg"], p["lru_bg"], p["lru_lam"])


MLA_TQ = 2048
MLA_TK = 256
MLA_NKV = TT // MLA_TK
MLA_UNROLL = 16
MLA_QCOL = 256


def _with_ones(vt):
    return jnp.concatenate([vt, jnp.ones((V_ROWS - vt.shape[0], vt.shape[1]), vt.dtype)], axis=0)


def _softmax_pv(s, vt, m, acc):
    m_new = jnp.maximum(m, jnp.max(s, axis=0, keepdims=True))
    p = jnp.exp2(s - m_new).astype(BF16)
    return m_new, acc * jnp.exp2(m - m_new) + _dot(vt, p)


MLA_NCOL = MLA_TQ // MLA_QCOL
MLA_COLS = [slice(c * MLA_QCOL, (c + 1) * MLA_QCOL) for c in range(MLA_NCOL)]


def _attend(q_cols, q_next_cols, k_ref, vt_ref, cur, nxt):
    tk = MLA_TK
    assert MLA_NKV % MLA_UNROLL == 1 and MLA_UNROLL % 2 == 0

    def keys(i):
        return k_ref[0, pl.ds(pl.multiple_of(i * tk, tk), tk), :]

    def values(i):
        return _with_ones(vt_ref[0, i])

    def group(j, carry):
        ms, accs = list(carry[0]), list(carry[1])
        a, b = cur, nxt
        for u in range(MLA_UNROLL):
            i = MLA_UNROLL * j + u
            k_next, v_cur = keys(i + 1), values(i)
            for c, sl in enumerate(MLA_COLS):
                b[:, sl] = _dot(k_next, q_cols[c])
                ms[c], accs[c] = _softmax_pv(a[:, sl], v_cur, ms[c], accs[c])
            a, b = b, a
        return tuple(ms), tuple(accs)

    init = (tuple(jnp.full((1, MLA_QCOL), NEG, F32) for _ in MLA_COLS),
            tuple(jnp.zeros((V_ROWS, MLA_QCOL), F32) for _ in MLA_COLS))
    ms, accs = lax.fori_loop(0, MLA_NKV // MLA_UNROLL, group, init)
    k_first, v_last = k_ref[0, 0:tk, :], values(MLA_NKV - 1)
    outs = []
    for c, sl in enumerate(MLA_COLS):
        nxt[:, sl] = _dot(k_first, q_next_cols[c])
        _, acc = _softmax_pv(cur[:, sl], v_last, ms[c], accs[c])
        outs.append(acc[:MLA_V] / acc[MLA_V:MLA_V + 1])
    return jnp.concatenate(outs, axis=1)


def _mla_kernel(q_ref, k_ref, vt_ref, o_ref, s_a, s_b):
    sc = _dot(k_ref[0, 0:CTX, :], q_ref[0, 0])
    _, acc = _softmax_pv(sc, _with_ones(vt_ref[0, 0]), jnp.full((1, CTX), NEG, F32),
                         jnp.zeros((V_ROWS, CTX), F32))
    o_ref[0, 0] = (acc[:MLA_V] / acc[MLA_V:MLA_V + 1]).astype(o_ref.dtype)

    n_tiles = SEQ // MLA_TQ
    assert n_tiles % 2 == 0 and MLA_QCOL == TM

    def q_tile(t):
        t = jnp.minimum(t, n_tiles - 1)
        return [q_ref[0, 1 + t * MLA_NCOL + c] for c in range(MLA_NCOL)]

    def store(t, o):
        for g in range(MLA_NCOL):
            o_ref[0, 1 + t * MLA_NCOL + g] = o[:, g * TM:(g + 1) * TM].astype(o_ref.dtype)

    q0 = q_tile(0)
    for c, sl in enumerate(MLA_COLS):
        s_a[:, sl] = _dot(k_ref[0, 0:MLA_TK, :], q0[c])

    def tile_pair(tt, _):
        t = 2 * tt
        store(t, _attend(q_tile(t), q_tile(t + 1), k_ref, vt_ref, s_a, s_b))
        store(t + 1, _attend(q_tile(t + 1), q_tile(t + 2), k_ref, vt_ref, s_b, s_a))
        return 0

    lax.fori_loop(0, n_tiles // 2, tile_pair, 0)


def _mla_call(qm, km, vt):
    assert MLA_TK == TM
    head_t_spec = pl.BlockSpec((1, NT, MLA_V, TM), lambda b, h: (b, 0, h, 0))
    return pl.pallas_call(
        _mla_kernel,
        grid=(B, MLA_H),
        in_specs=[pl.BlockSpec((1, NT, HEAD_SLOT, TM), lambda b, h: (b, 0, h, 0)),
                  pl.BlockSpec((1, TT, HEAD_SLOT), lambda b, h: (b, 0, h)),
                  head_t_spec],
        out_specs=head_t_spec,
        out_shape=jax.ShapeDtypeStruct((B, NT, MLA_H * MLA_V, TM), BF16),
        scratch_shapes=[pltpu.VMEM((MLA_TK, MLA_TQ), F32), pltpu.VMEM((MLA_TK, MLA_TQ), F32)],
        compiler_params=_cparams(("parallel", "arbitrary")),
        name="mla_attn",
    )(qm, km, vt)


def _na_kernel(q_ref, k_ref, vt_ref, tab_ref, o_ref):
    j = pl.program_id(1)
    q_all = q_ref[0, 0]
    row_head = jnp.right_shift(lax.broadcasted_iota(jnp.int32, q_all.shape, 0), 6)
    q_heads = [jnp.where(row_head == h, q_all, jnp.zeros_like(q_all)) for h in range(NA_H)]

    def values(g, h):
        return vt_ref[0, g, h * NA_DH:(h + 1) * NA_DH, :]

    def finish(h, s, vt):
        m = jnp.max(s, axis=0, keepdims=True)
        acc = _dot(_with_ones(vt), jnp.exp(s - m).astype(BF16))
        o_ref[0, 0, h * NA_DH:(h + 1) * NA_DH, :] = (acc[:NA_DH] / acc[NA_DH:NA_DH + 1]).astype(o_ref.dtype)

    @pl.when(j == 0)
    def _():
        for h in range(NA_H):
            finish(h, _dot(k_ref[0, 0], q_heads[h]), values(0, h))

    @pl.when(j > 0)
    def _():
        jj = j - 1
        g0 = jnp.clip(jj - 1, 0, NA_NGROUPS - NA_BAND_G)
        variant = jnp.where(jj == 0, 0, jnp.where(jj == NA_NGROUPS - 1, 2, 1))
        band = k_ref[0, pl.ds(1 + g0, NA_BAND_G)].reshape(NA_BAND_G * TM, NA_W)
        keys = jnp.concatenate([k_ref[0, 0], band], axis=0)
        scores, probs = {}, {}
        for t in range(NA_H + 2):
            if t < NA_H:
                scores[t] = _dot(keys, q_heads[t])
            h = t - 1
            if 0 <= h < NA_H:
                s = scores.pop(h)
                s = jnp.concatenate([s[:TM], s[TM:] + tab_ref[variant, h]], axis=0)
                probs[h] = jnp.exp(s - jnp.max(s, axis=0, keepdims=True)).astype(BF16)
            h = t - 2
            if 0 <= h < NA_H:
                vt = jnp.concatenate([values(0, h)] + [values(1 + g0 + g, h) for g in range(NA_BAND_G)], axis=1)
                acc = _dot(_with_ones(vt), probs.pop(h))
                o_ref[0, 0, h * NA_DH:(h + 1) * NA_DH, :] = (
                    acc[:NA_DH] / acc[NA_DH:NA_DH + 1]).astype(o_ref.dtype)


def _na_call(q, k, vt, tab, layer):
    whole = pl.BlockSpec((1, NT, TM, NA_W), lambda b, j: (b, 0, 0, 0))
    group_t = pl.BlockSpec((1, 1, NA_W, TM), lambda b, j: (b, j, 0, 0))
    return pl.pallas_call(
        _na_kernel,
        grid=(B, NT),
        in_specs=[group_t, whole, whole, _layer_spec((3, NA_H, NA_BAND_G * TM, TM), layer)],
        out_specs=group_t,
        out_shape=jax.ShapeDtypeStruct((B, NT, NA_W, TM), BF16),
        compiler_params=_cparams(("parallel", "arbitrary")),
        name="na_attn",
    )(q, k, vt, tab)


def _rope_tables():
    t = jnp.arange(SEQ)
    pos = jnp.stack([t // GRID_W, t % GRID_W], axis=-1).astype(F32)
    half = MLA_ROPE // 2
    inv = ROPE_BASE ** (-jnp.arange(0, half, 2, dtype=F32) / half)
    ang = pos[:, :, None] * inv
    cos, sin = jnp.cos(ang), jnp.sin(ang)
    ones = jnp.ones((SEQ, MLA_NOPE), F32)
    c = jnp.concatenate([ones, cos[:, 0], cos[:, 0], cos[:, 1], cos[:, 1], ones[:, :32]], axis=-1)
    s = jnp.concatenate([0 * ones, -sin[:, 0], sin[:, 0], -sin[:, 1], sin[:, 1], 0 * ones[:, :32]], axis=-1)
    ctx_c = jnp.ones((CTX, LANES), F32)
    ctx_s = jnp.zeros((CTX, LANES), F32)
    return jnp.concatenate([ctx_c, c]), jnp.concatenate([ctx_s, s])


ROPE_PARTNER = np.concatenate([np.arange(8, 16), np.arange(0, 8), np.arange(24, 32), np.arange(16, 24)])


def _group_ones(width, group):
    g = np.arange(width) // group
    return jnp.asarray(g[:, None] == g[None, :], BF16)


def _na_window_pattern():
    n_band = NA_BAND_G * NA_QROWS
    i = np.arange(n_band)[:, None]
    a = np.arange(NA_QROWS)[None, :]

    def pattern(jj):
        g0 = min(max(jj - 1, 0), NA_NGROUPS - NA_BAND_G)
        krow = NA_QROWS * g0 + i
        qrow = NA_QROWS * jj + a
        r0q = np.clip(qrow - NA_WR // 2, 0, GRID_H - NA_WR)
        return (krow >= r0q) & (krow < r0q + NA_WR), krow - qrow + NA_WR - 1

    interior = [pattern(jj) for jj in range(1, NA_NGROUPS - 1)]
    assert all((v == interior[0][0]).all() and (r[v] == interior[0][1][v]).all() for v, r in interior)
    pats = [pattern(0), interior[0], pattern(NA_NGROUPS - 1)]
    valid = np.stack([p[0] for p in pats])
    row_off = np.stack([p[1] for p in pats])
    assert (row_off[valid] >= 0).all() and (row_off[valid] <= 2 * NA_WR - 2).all()
    return valid, row_off


def _na_tables(rpb):
    kc = jnp.arange(GRID_W)[:, None]
    qc = jnp.arange(GRID_W)[None, :]
    win = jnp.clip(qc - NA_WC // 2, 0, GRID_W - NA_WC)
    ok_col = (kc >= win) & (kc < win + NA_WC)
    col_off = kc - qc + NA_WC - 1
    pick_col = ((col_off[None] == jnp.arange(2 * NA_WC - 1)[:, None, None]) & ok_col[None]).astype(F32)
    valid, row_off = _na_window_pattern()
    pick_row = ((row_off[..., None] == np.arange(2 * NA_WR - 1)) & valid[..., None]).astype(np.float32)
    eye_a = jnp.eye(NA_QROWS, dtype=F32)
    pick_col = (eye_a[:, None, None, :, None] * pick_col[None, :, :, None, :]).reshape(
        NA_QROWS, 2 * NA_WC - 1, GRID_W, TM)
    rows = jnp.einsum("vian,dhnj->dvhiaj", pick_row, rpb, precision=lax.Precision.HIGHEST)
    tab = jnp.einsum("dvhiaj,ajkz->dvhikz", rows, pick_col, precision=lax.Precision.HIGHEST)
    keep = (valid[:, None, :, None, :, None] & ok_col[None, None, None, :, None, :]).reshape(
        3, 1, NA_BAND_G * NA_QROWS, GRID_W, TM)
    tab = jnp.where(keep, tab, NEG)
    return tab.reshape(DEPTH, 3, NA_H, NA_BAND_G * TM, TM)


def _block_diag(w):
    w = w.reshape(DEPTH, 2, LRU_W // LANES, 2, LRU_BW, LRU_BW)
    z = jnp.zeros_like(w[:, :, :, 0])
    top = jnp.concatenate([w[:, :, :, 0], z], axis=-1)
    bot = jnp.concatenate([z, w[:, :, :, 1]], axis=-1)
    return jnp.concatenate([top, bot], axis=-2)


def _prepare(w_in, w_out, lru_conv_b, lru_w_a, lru_b_a, lru_w_x, lru_b_x, mla_w_uq, mla_w_ukv,
             mla_q_gain, mla_k_gain, na_q_gain, na_k_gain):
    z = lambda n: jnp.zeros((DEPTH, D, n), F32)
    kr = w_in[..., 1152:1184]
    w_in_ext = jnp.concatenate(
        [w_in[..., :1152], z(MLA_NOPE), kr, kr[..., ROPE_PARTNER], w_in[..., 1184:]], axis=-1).astype(BF16)
    w_uq = mla_w_uq.reshape(DEPTH, MLA_QR, MLA_H, MLA_QK)
    slot_pad = ((0, 0), (0, 0), (0, 0), (0, HEAD_SLOT - MLA_QK))
    zq = jnp.zeros((DEPTH, MLA_QR, MLA_H, MLA_NOPE), F32)
    w_uq_partner = jnp.concatenate([zq, w_uq[..., MLA_NOPE + ROPE_PARTNER]], axis=-1)
    w_uq = jnp.concatenate([jnp.pad(w_uq, slot_pad).reshape(DEPTH, MLA_QR, -1),
                            jnp.pad(w_uq_partner, slot_pad).reshape(DEPTH, MLA_QR, -1)], axis=-1)
    w_ukv = mla_w_ukv.reshape(DEPTH, MLA_KVR, MLA_H, MLA_NOPE + MLA_V)
    w_uk = jnp.pad(w_ukv[..., :MLA_NOPE], ((0, 0), (0, 0), (0, 0), (0, HEAD_SLOT - MLA_NOPE)))
    w_ukv_ext = jnp.concatenate([w_uk.reshape(DEPTH, MLA_KVR, -1),
                                 w_ukv[..., MLA_NOPE:].reshape(DEPTH, MLA_KVR, -1)], axis=-1)

    def pad_gain(g):
        partner = jnp.concatenate([jnp.zeros((DEPTH, MLA_NOPE), F32), g[:, MLA_NOPE + ROPE_PARTNER]], axis=-1)
        both = jnp.stack([g, partner], axis=1)
        return jnp.pad(both, ((0, 0), (0, 0), (0, HEAD_SLOT - MLA_QK)))
    tile_gain = lambda g: jnp.tile(g, (1, NA_H)).reshape(DEPTH, 1, NA_W)
    wa, wx = _block_diag(lru_w_a), _block_diag(lru_w_x)
    wg = jnp.concatenate([wa[:, 0], wx[:, 0], wa[:, 1], wx[:, 1]], axis=-1).astype(BF16)
    ba = lru_b_a.reshape(DEPTH, 2, LRU_W // LANES, 1, LANES)
    bx = lru_b_x.reshape(DEPTH, 2, LRU_W // LANES, 1, LANES)
    bg = jnp.concatenate([ba[:, 0], bx[:, 0], ba[:, 1], bx[:, 1]], axis=-1)
    w_out = w_out.astype(BF16)
    return dict(
        w_in=w_in_ext, w_uq=w_uq.astype(BF16), w_ukv=w_ukv_ext.astype(BF16),
        mq_gain=pad_gain(mla_q_gain), mk_gain=pad_gain(mla_k_gain),
        nq_gain=tile_gain(na_q_gain), nk_gain=tile_gain(na_k_gain),
        lru_wg=wg, lru_bg=bg, conv_b=lru_conv_b.reshape(DEPTH, 1, LRU_W),
        w_out_lru=w_out[:, :LRU_W], w_out_mla=w_out[:, LRU_W:LRU_W + MLA_H * MLA_V],
        w_out_na=w_out[:, LRU_W + MLA_H * MLA_V:],
    )


def kernel(x, c, ctx, c_ctx, w_mod, b_mod, norm_ffn1, ffn1_w_in, ffn1_w_out, norm_mix, w_in, w_out, lru_conv_w, lru_conv_b, lru_w_a, lru_b_a, lru_w_x, lru_b_x, lru_lambda, mla_q_norm, mla_w_uq, mla_kv_norm, mla_w_ukv, mla_q_gain, mla_k_gain, na_q_gain, na_k_gain, na_rpb, norm_ffn2, ffn2_w_in, ffn2_w_out):
    cc = jnp.concatenate([c, c_ctx[None], jnp.zeros((8 - B - 1, D), F32)], axis=0)
    mod = _mod_call(cc, w_mod, b_mod)
    mod = mod.reshape(DEPTH, 8, 3, 3, D)
    mod = jnp.stack([jnp.broadcast_to(mod[:, B:B + 1], (DEPTH, B, 3, 3, D)), mod[:, :B]], axis=2)

    prm = _prepare(w_in, w_out, lru_conv_b, lru_w_a, lru_b_a, lru_w_x, lru_b_x, mla_w_uq, mla_w_ukv,
                   mla_q_gain, mla_k_gain, na_q_gain, na_k_gain)
    prm.update(norm_mix=norm_mix.reshape(DEPTH, 1, D), q_norm=mla_q_norm.reshape(DEPTH, 1, MLA_QR),
               kv_norm=mla_kv_norm.reshape(DEPTH, 1, MLA_KVR), conv_w=lru_conv_w, lru_lam=lru_lambda)
    f1_in, f1_out = ffn1_w_in.astype(BF16), ffn1_w_out.astype(BF16)
    f2_in, f2_out = ffn2_w_in.astype(BF16), ffn2_w_out.astype(BF16)
    g1, g2 = norm_ffn1.reshape(DEPTH, 1, D), norm_ffn2.reshape(DEPTH, 1, D)
    rope_tabs = _rope_tables() + (_group_ones(MXU_TILE, HEAD_SLOT), _group_ones(MXU_TILE, NA_DH))
    na_tabs = _na_tables(na_rpb)

    h = jnp.concatenate([ctx, x], axis=1)
    for l in range(DEPTH):
        p = {k: v[l] for k, v in prm.items()}
        h = _ffn_call(h, mod[l], g1[l], f1_in, f1_out, l, 0, "ffn1")
        lx, lg, qm, km, vm, nq, nk, nv = _mixin_call(h, mod[l], p, rope_tabs)
        ylru = _lru_call(lx, lg, p)
        ymla = _mla_call(qm, km, vm)
        yna = _na_call(nq, nk.reshape(B, NT, TM, NA_W), nv, na_tabs, l)
        h = _post_call(h, mod[l], ylru, ymla, yna, p, g2[l], f2_in, f2_out, l)
    return h[:, CTX:]
```

```python
import functools
import math

import jax
import jax.numpy as jnp
import numpy as np
from jax import lax
from jax.experimental import pallas as pl
from jax.experimental.pallas import tpu as pltpu

F32 = jnp.float32
BF16 = jnp.bfloat16

D = 1024
B = 4
SEQ = 8192
DEPTH = 4
GRID_W = 64
GRID_H = SEQ // GRID_W
CTX = 256
TT = CTX + SEQ
D_FF = 2816
EPS = 1e-6

LRU_W = 384
LRU_BW = 64
LRU_C = 8.0
CONV_W = 4
CONV_LEFT = 2

MLA_H = 6
MLA_QR = 256
MLA_KVR = 128
MLA_NOPE = 64
MLA_ROPE = 32
MLA_QK = MLA_NOPE + MLA_ROPE
MLA_V = 64
ROPE_BASE = 10000.0

NA_H = 4
NA_DH = 64
NA_W = NA_H * NA_DH
NA_WR = 8
NA_WC = 16

LANES = 128
HEAD_SLOT = LANES
IN_EXT = 2048
TM = 256
NT = TT // TM
V_ROWS = 80
NEG = -1e30
VMEM_LIMIT = 56 * 1024 * 1024

NA_QROWS = 4
NA_GROUP = NA_QROWS * GRID_W
NA_NGROUPS = SEQ // NA_GROUP
NA_BAND_G = 3

MLA_QSCALE = MLA_QK ** -0.5 * math.log2(math.e)
NA_QSCALE = NA_DH ** -0.5


def _cparams(sem):
    return pltpu.CompilerParams(dimension_semantics=sem, vmem_limit_bytes=VMEM_LIMIT)


def _dot(a, b):
    return jnp.dot(a, b, preferred_element_type=F32)


def _dot_nt(a, b):
    return lax.dot_general(a, b, (((1,), (1,)), ((), ())), preferred_element_type=F32)


def _rms(x, n=None):
    n = x.shape[-1] if n is None else n
    return x * lax.rsqrt(jnp.sum(x * x, axis=-1, keepdims=True) * (1.0 / n) + EPS)


def _mod_kernel(c_ref, w_ref, b_ref, o_ref):
    s = jax.nn.silu(c_ref[...]).astype(BF16)
    o_ref[0] = _dot(s, w_ref[0].astype(BF16)) + b_ref[0]


def _mod_call(cc, w_mod, b_mod):
    n = w_mod.shape[-1]
    tn = n // 4
    return pl.pallas_call(
        _mod_kernel,
        grid=(DEPTH, n // tn),
        in_specs=[
            pl.BlockSpec((8, D), lambda l, j: (0, 0)),
            pl.BlockSpec((1, D, tn), lambda l, j: (l, 0, j)),
            pl.BlockSpec((1, 1, tn), lambda l, j: (l, 0, j)),
        ],
        out_specs=pl.BlockSpec((1, 8, tn), lambda l, j: (l, 0, j)),
        out_shape=jax.ShapeDtypeStruct((DEPTH, 8, n), F32),
        compiler_params=_cparams(("arbitrary", "arbitrary")),
        name="mod",
    )(cc, w_mod, b_mod.reshape(DEPTH, 1, n))


TOK = 768
GROUPS = TOK // TM
NTOK = TT // TOK


def _norm_mod(x, g, mod_ref):
    row = pl.program_id(1) * TOK + lax.broadcasted_iota(jnp.int32, (x.shape[0], 1), 0)
    is_ctx = row < CTX
    mod_c, mod_l = mod_ref[0, 0, 0], mod_ref[0, 1, 0]
    pick = lambda r: jnp.where(is_ctx, mod_c[r:r + 1], mod_l[r:r + 1])
    xn = _rms(x) * g
    return xn * (1.0 + pick(1)) + pick(0), pick(2)


MXU_TILE = 256
FFN_SPLIT = (D_FF // MXU_TILE + 1) // 2 * MXU_TILE
FFN_CHUNKS = ((0, FFN_SPLIT), (FFN_SPLIT, D_FF))


def _ffn_body(x, mod_ref, g_ref, win_ref, wout_ref, chunks):
    xm, gate_row = _norm_mod(x, g_ref[...], mod_ref)
    xb = xm.astype(BF16)
    acc = jnp.zeros((x.shape[0], D), F32)
    for lo, hi in chunks:
        gate = _dot(xb, win_ref[:, lo:hi])
        up = _dot(xb, win_ref[:, D_FF + lo:D_FF + hi])
        a = (jax.nn.silu(gate) * up).astype(BF16)
        acc = acc + _dot(a, wout_ref[lo:hi, :])
    return x + 0.5 * gate_row * acc


def _ffn_kernel(x_ref, mod_ref, g_ref, win_ref, wout_ref, o_ref):
    o_ref[0] = _ffn_body(x_ref[0], mod_ref, g_ref, win_ref, wout_ref, FFN_CHUNKS)


def _mod_spec(group):
    return pl.BlockSpec((1, 2, 1, 3, D), lambda b, i: (b, 0, group, 0, 0))


def _tok_spec(width):
    return pl.BlockSpec((1, TOK, width), lambda b, i: (b, i, 0))


def _const_spec(shape):
    return pl.BlockSpec(shape, lambda b, i: (0,) * len(shape), pipeline_mode=pl.Buffered(1))


def _layer_spec(shape, layer):
    return pl.BlockSpec((None,) + shape, lambda b, i: (layer,) + (0,) * len(shape),
                        pipeline_mode=pl.Buffered(1))


def _ffn_call(h, mod, g, w_in, w_out, layer, group, name):
    return pl.pallas_call(
        _ffn_kernel,
        grid=(B, NTOK),
        in_specs=[_tok_spec(D), _mod_spec(group), _const_spec((1, D)),
                  _layer_spec((D, 2 * D_FF), layer), _layer_spec((D_FF, D), layer)],
        out_specs=_tok_spec(D),
        out_shape=jax.ShapeDtypeStruct((B, TT, D), F32),
        compiler_params=_cparams(("parallel", "arbitrary")),
        name=name,
    )(h, mod, g, w_in, w_out)


def _group_ssq(x, ones_ref):
    sq = x * x
    hi = sq.astype(BF16)
    lo = (sq - hi.astype(F32)).astype(BF16)
    ones = ones_ref[...]
    tiles = [slice(c, c + MXU_TILE) for c in range(0, x.shape[1], MXU_TILE)]
    return jnp.concatenate([_dot(hi[:, t], ones) + _dot(lo[:, t], ones) for t in tiles], axis=1)


def _mixin_kernel(x_ref, mod_ref, g_ref, win_ref, qn_ref, wuq_ref, kvn_ref, wukv_ref,
                  mqg_ref, mkg_ref, nqg_ref, nkg_ref, rc_ref, rs_ref, ones_h_ref, ones_n_ref,
                  lx_ref, lg_ref, qm_ref, km_ref, vm_ref, nq_ref, nk_ref, nv_ref):
    x = x_ref[0]
    xb = _norm_mod(x, g_ref[...], mod_ref)[0].astype(BF16)
    y = _dot(xb, win_ref[...])
    lx_ref[0] = y[:, 0:LRU_W]
    lg_ref[0] = y[:, LRU_W:2 * LRU_W]
    cq = y[:, 768:1024]
    ckv = y[:, 1024:1152]
    kr_slot = y[:, 1152:1280]
    rc, rs = rc_ref[...], rs_ref[...]
    n_slot = MLA_H * HEAD_SLOT

    q2 = _dot((_rms(cq) * qn_ref[...]).astype(BF16), wuq_ref[...])
    q_raw = q2[:, :n_slot]
    inv_q = lax.rsqrt(_group_ssq(q_raw, ones_h_ref) * (1.0 / MLA_QK) + EPS)
    cos_q = rc * (mqg_ref[0:1] * MLA_QSCALE)
    sin_q = rs * (mqg_ref[1:2] * MLA_QSCALE)

    kv = _dot((_rms(ckv) * kvn_ref[...]).astype(BF16), wukv_ref[...])
    lane = lax.broadcasted_iota(jnp.int32, kr_slot.shape, 1)
    kr = jnp.where(lane < MLA_QK, kr_slot, 0.0)
    kr_partner = pltpu.roll(kr_slot, LANES - MLA_ROPE, 1)
    k_raw = kv[:, :n_slot] + jnp.concatenate([kr] * MLA_H, axis=1)
    inv_k = lax.rsqrt(_group_ssq(k_raw, ones_h_ref) * (1.0 / MLA_QK) + EPS)
    cos_k = rc * mkg_ref[0:1]
    sin_k = kr_partner * (rs * mkg_ref[1:2])
    q_heads = []
    for h in range(MLA_H):
        sl = slice(h * HEAD_SLOT, (h + 1) * HEAD_SLOT)
        qh = q_raw[:, sl] * cos_q + q2[:, n_slot + h * HEAD_SLOT:n_slot + (h + 1) * HEAD_SLOT] * sin_q
        q_heads.append(qh * inv_q[:, sl])
        km_ref[0, :, sl] = ((k_raw[:, sl] * cos_k + sin_k) * inv_k[:, sl]).astype(BF16)
    q_all = jnp.concatenate(q_heads, axis=1)
    for s in range(GROUPS):
        qm_ref[0, s] = q_all[s * TM:(s + 1) * TM].T.astype(BF16)
        vm_ref[0, s] = kv[s * TM:(s + 1) * TM, n_slot:].T.astype(BF16)

    nq = y[:, 1280:1536]
    nq = nq * lax.rsqrt(_group_ssq(nq, ones_n_ref) * (1.0 / NA_DH) + EPS) * (nqg_ref[...] * NA_QSCALE)
    nk = y[:, 1536:1792]
    nk_ref[0] = (nk * lax.rsqrt(_group_ssq(nk, ones_n_ref) * (1.0 / NA_DH) + EPS) * nkg_ref[...]).astype(BF16)
    for s in range(GROUPS):
        nq_ref[0, s] = nq[s * TM:(s + 1) * TM].T.astype(BF16)
        nv_ref[0, s] = y[s * TM:(s + 1) * TM, 1792:2048].T.astype(BF16)


def _tile_t_spec(rows):
    return pl.BlockSpec((1, GROUPS, rows, TM), lambda b, i: (b, i, 0, 0))


def _mixin_call(h, mod, p, tabs):
    n_slot = MLA_H * HEAD_SLOT
    tok = lambda w, dt: (_tok_spec(w), jax.ShapeDtypeStruct((B, TT, w), dt))
    tile_t = lambda rows: (_tile_t_spec(rows), jax.ShapeDtypeStruct((B, NT, rows, TM), BF16))
    outs = [tok(LRU_W, F32), tok(LRU_W, F32), tile_t(n_slot), tok(n_slot, BF16), tile_t(MLA_H * MLA_V),
            tile_t(NA_W), tok(NA_W, BF16), tile_t(NA_W)]
    out_specs, out_shape = [o[0] for o in outs], [o[1] for o in outs]
    rope_spec = pl.BlockSpec((TOK, LANES), lambda b, i: (i, 0))
    return pl.pallas_call(
        _mixin_kernel,
        grid=(B, NTOK),
        in_specs=[_tok_spec(D), _mod_spec(1), _const_spec((1, D)), _const_spec((D, IN_EXT)),
                  _const_spec((1, MLA_QR)), _const_spec((MLA_QR, 2 * n_slot)),
                  _const_spec((1, MLA_KVR)), _const_spec((MLA_KVR, n_slot + MLA_H * MLA_V)),
                  _const_spec((2, HEAD_SLOT)), _const_spec((2, HEAD_SLOT)),
                  _const_spec((1, NA_W)), _const_spec((1, NA_W)),
                  rope_spec, rope_spec, _const_spec((MXU_TILE, MXU_TILE)), _const_spec((MXU_TILE, MXU_TILE))],
        out_specs=out_specs,
        out_shape=out_shape,
        compiler_params=_cparams(("parallel", "arbitrary")),
        name="mix_in",
    )(h, mod, p["norm_mix"], p["w_in"], p["q_norm"], p["w_uq"], p["kv_norm"], p["w_ukv"],
      p["mq_gain"], p["mk_gain"], p["nq_gain"], p["nk_gain"], *tabs)


def _dot_tn(a, b):
    return lax.dot_general(a, b, (((0,), (0,)), ((), ())), preferred_element_type=F32)


def _ffn_chunks(tiles):
    return tuple((lo, min(lo + tiles * MXU_TILE, D_FF)) for lo in range(0, D_FF, tiles * MXU_TILE))


POST_CHUNKS = _ffn_chunks(3)


def _post_kernel(h_ref, modm_ref, ylru_ref, ymla_ref, yna_ref, w1_ref, w2_ref, w3_ref,
                 modf_ref, g_ref, win_ref, wout_ref, o_ref):
    att = jnp.concatenate([_dot_tn(ymla_ref[0, s], w2_ref[...]) + _dot_tn(yna_ref[0, s], w3_ref[...])
                           for s in range(GROUPS)], axis=0)
    y = _dot(ylru_ref[0], w1_ref[...]) + att
    row = pl.program_id(1) * TOK + lax.broadcasted_iota(jnp.int32, (TOK, 1), 0)
    gate = jnp.where(row < CTX, modm_ref[0, 0, 0][2:3], modm_ref[0, 1, 0][2:3])
    h = h_ref[0] + gate * y
    o_ref[0] = _ffn_body(h, modf_ref, g_ref, win_ref, wout_ref, POST_CHUNKS)


def _post_call(h, mod, ylru, ymla, yna, p, g, w_in, w_out, layer):
    return pl.pallas_call(
        _post_kernel,
        grid=(B, NTOK),
        in_specs=[_tok_spec(D), _mod_spec(1), _tok_spec(LRU_W), _tile_t_spec(MLA_H * MLA_V), _tile_t_spec(NA_W),
                  _const_spec((LRU_W, D)), _const_spec((MLA_H * MLA_V, D)), _const_spec((NA_W, D)),
                  _mod_spec(2), _const_spec((1, D)),
                  _layer_spec((D, 2 * D_FF), layer), _layer_spec((D_FF, D), layer)],
        out_specs=_tok_spec(D),
        out_shape=jax.ShapeDtypeStruct((B, TT, D), F32),
        compiler_params=_cparams(("parallel", "arbitrary")),
        name="post",
    )(h, mod, ylru, ymla, yna, p["w_out_lru"], p["w_out_mla"], p["w_out_na"], mod, g, w_in, w_out)


LRU_CHUNK = 256
LRU_PAD = 8
LRU_XROWS = TT + 3 * LRU_PAD
SUB = 8


def _scan_block(a, u, carry, reverse):
    row = lax.broadcasted_iota(jnp.int32, a.shape, 0)
    for s in (1, 2, 4):
        if reverse:
            keep = row < SUB - s
            a_s = jnp.where(keep, pltpu.roll(a, SUB - s, 0), 1.0)
            u_s = jnp.where(keep, pltpu.roll(u, SUB - s, 0), 0.0)
        else:
            keep = row >= s
            a_s = jnp.where(keep, pltpu.roll(a, s, 0), 1.0)
            u_s = jnp.where(keep, pltpu.roll(u, s, 0), 0.0)
        u = a * u_s + u
        a = a * a_s
    h = a * carry + u
    edge = h[0:1] if reverse else h[SUB - 1:SUB]
    return h, jnp.broadcast_to(edge, a.shape)


LRU_SLICE = 64
LRU_NCHUNK = TT // LRU_CHUNK


def _lru_kernel(lx_ref, lg_ref, cw_ref, cb_ref, wg_ref, bg_ref, lam_ref, y_ref, xpad, hf, hb, au0, au1):
    zeros_pad = jnp.zeros((LRU_PAD, LANES), F32)
    xpad[0:LRU_PAD] = zeros_pad
    xpad[LRU_PAD:LRU_PAD + CTX] = lx_ref[0, 0:CTX]
    xpad[LRU_PAD + CTX:2 * LRU_PAD + CTX] = zeros_pad
    xpad[2 * LRU_PAD + CTX:2 * LRU_PAD + TT] = lx_ref[0, CTX:TT]
    xpad[2 * LRU_PAD + TT:3 * LRU_PAD + TT] = zeros_pad

    neg_c_softplus = -LRU_C * jax.nn.softplus(-lam_ref[...])
    n_slices = LRU_CHUNK // LRU_SLICE
    blocks_per_slice = LRU_SLICE // SUB

    def fwd_chunk(i):
        return i

    def bwd_chunk(i):
        return jnp.where(i == 0, 0, LRU_NCHUNK - i)

    def gates(c, d, s, buf):
        base = pl.multiple_of(jnp.where(c == 0, LRU_PAD, 2 * LRU_PAD + c * LRU_CHUNK), SUB) + s * LRU_SLICE
        xc = jnp.broadcast_to(cb_ref[...], (LRU_SLICE, LANES))
        for j in range(CONV_W):
            xc = xc + xpad[pl.ds(base + (j - CONV_LEFT), LRU_SLICE), :] * cw_ref[j:j + 1, :]
        cols = slice(d * 2 * LANES, (d + 1) * 2 * LANES)
        pre = _dot(xc.astype(BF16), wg_ref[0, :, cols]) + bg_ref[0, :, cols]
        r = jax.nn.sigmoid(pre[:, :LANES])
        gate_in = jax.nn.sigmoid(pre[:, LANES:])
        log_a = r * neg_c_softplus[d:d + 1]
        rows = slice(s * LRU_SLICE, (s + 1) * LRU_SLICE)
        buf[2 * d, rows, :] = jnp.exp(log_a)
        th = jnp.tanh(log_a)
        buf[2 * d + 1, rows, :] = jnp.sqrt(-2.0 * th / (1.0 - th)) * (gate_in * xc)

    def scan(c, d, s, buf, carry, dst):
        row0 = pl.multiple_of(c * LRU_CHUNK, SUB)
        for k in range(blocks_per_slice):
            blk = s * blocks_per_slice + k
            if d == 1:
                blk = LRU_CHUNK // SUB - 1 - blk
            rows = slice(blk * SUB, (blk + 1) * SUB)
            h, carry = _scan_block(buf[2 * d, rows, :], buf[2 * d + 1, rows, :], carry, d == 1)
            dst[pl.ds(row0 + blk * SUB, SUB), :] = h
        return carry

    def step(i, cur, nxt, carry):
        cf, cb = carry
        for s in range(n_slices):
            if nxt is not None:
                gates(fwd_chunk(i + 1), 0, s, nxt)
                gates(bwd_chunk(i + 1), 1, s, nxt)
            cf = scan(fwd_chunk(i), 0, s, cur, cf, hf)
            cb = scan(bwd_chunk(i), 1, s, cur, cb, hb)
        return cf, cb

    for s in range(n_slices):
        gates(0, 0, s, au0)
        gates(0, 1, s, au0)

    def pair(j, carry):
        carry = step(2 * j, au0, au1, carry)
        return step(2 * j + 1, au1, au0, carry)

    assert LRU_NCHUNK % 2 == 1
    zero = jnp.zeros((SUB, LANES), F32)
    carry = lax.fori_loop(0, LRU_NCHUNK // 2, pair, (zero, zero))
    step(LRU_NCHUNK - 1, au0, None, carry)

    def finish(c, _):
        rows = pl.ds(pl.multiple_of(c * LRU_CHUNK, LRU_CHUNK), LRU_CHUNK)
        y_ref[0, rows, :] = ((hf[rows, :] + hb[rows, :]) * jax.nn.gelu(lg_ref[0, rows, :])).astype(y_ref.dtype)
        return 0

    lax.fori_loop(0, LRU_NCHUNK, finish, 0)


def _lru_call(lx, lg, p):
    nl = LRU_W // LANES
    seq_spec = pl.BlockSpec((1, TT, LANES), lambda b, j: (b, 0, j))
    return pl.pallas_call(
        _lru_kernel,
        grid=(B, nl),
        in_specs=[seq_spec, seq_spec,
                  pl.BlockSpec((CONV_W, LANES), lambda b, j: (0, j)),
                  pl.BlockSpec((1, LANES), lambda b, j: (0, j)),
                  pl.BlockSpec((1, LANES, 4 * LANES), lambda b, j: (j, 0, 0)),
                  pl.BlockSpec((1, 1, 4 * LANES), lambda b, j: (j, 0, 0)),
                  pl.BlockSpec((2, LANES), lambda b, j: (0, j))],
        out_specs=seq_spec,
        out_shape=jax.ShapeDtypeStruct((B, TT, LRU_W), BF16),
        scratch_shapes=[pltpu.VMEM((LRU_XROWS, LANES), F32), pltpu.VMEM((TT, LANES), F32),
                        pltpu.VMEM((TT, LANES), F32),
                        pltpu.VMEM((4, LRU_CHUNK, LANES), F32), pltpu.VMEM((4, LRU_CHUNK, LANES), F32)],
        compiler_params=_cparams(("parallel", "arbitrary")),
        name="lru",
    )(lx, lg, p["conv_w"], p["conv_b"], p["lru_wg"], p["lru_bg"], p["lru_lam"])


MLA_TQ = 2048
MLA_TK = 256
MLA_NKV = TT // MLA_TK
MLA_UNROLL = 8
MLA_QCOL = 256


def _with_ones(vt):
    return jnp.concatenate([vt, jnp.ones((V_ROWS - vt.shape[0], vt.shape[1]), vt.dtype)], axis=0)


def _softmax_pv(s, vt, m, acc):
    m_new = jnp.maximum(m, jnp.max(s, axis=0, keepdims=True))
    p = jnp.exp2(s - m_new).astype(BF16)
    return m_new, acc * jnp.exp2(m - m_new) + _dot(vt, p)


MLA_NCOL = MLA_TQ // MLA_QCOL
MLA_COLS = [slice(c * MLA_QCOL, (c + 1) * MLA_QCOL) for c in range(MLA_NCOL)]


def _attend(q_cols, q_next_cols, k_ref, vt_ref, cur, nxt):
    tk = MLA_TK
    assert MLA_NKV % MLA_UNROLL == 1 and MLA_UNROLL % 2 == 0

    def keys(i):
        return k_ref[0, pl.ds(pl.multiple_of(i * tk, tk), tk), :]

    def values(i):
        return _with_ones(vt_ref[0, i])

    def group(j, carry):
        ms, accs = list(carry[0]), list(carry[1])
        a, b = cur, nxt
        for u in range(MLA_UNROLL):
            i = MLA_UNROLL * j + u
            k_next, v_cur = keys(i + 1), values(i)
            for c, sl in enumerate(MLA_COLS):
                b[:, sl] = _dot(k_next, q_cols[c])
                ms[c], accs[c] = _softmax_pv(a[:, sl], v_cur, ms[c], accs[c])
            a, b = b, a
        return tuple(ms), tuple(accs)

    init = (tuple(jnp.full((1, MLA_QCOL), NEG, F32) for _ in MLA_COLS),
            tuple(jnp.zeros((V_ROWS, MLA_QCOL), F32) for _ in MLA_COLS))
    ms, accs = lax.fori_loop(0, MLA_NKV // MLA_UNROLL, group, init)
    k_first, v_last = k_ref[0, 0:tk, :], values(MLA_NKV - 1)
    outs = []
    for c, sl in enumerate(MLA_COLS):
        nxt[:, sl] = _dot(k_first, q_next_cols[c])
        _, acc = _softmax_pv(cur[:, sl], v_last, ms[c], accs[c])
        outs.append(acc[:MLA_V] / acc[MLA_V:MLA_V + 1])
    return jnp.concatenate(outs, axis=1)


def _mla_kernel(q_ref, k_ref, vt_ref, o_ref, s_a, s_b):
    sc = _dot(k_ref[0, 0:CTX, :], q_ref[0, 0])
    _, acc = _softmax_pv(sc, _with_ones(vt_ref[0, 0]), jnp.full((1, CTX), NEG, F32),
                         jnp.zeros((V_ROWS, CTX), F32))
    o_ref[0, 0] = (acc[:MLA_V] / acc[MLA_V:MLA_V + 1]).astype(o_ref.dtype)

    n_tiles = SEQ // MLA_TQ
    assert n_tiles % 2 == 0 and MLA_QCOL == TM

    def q_tile(t):
        t = jnp.minimum(t, n_tiles - 1)
        return [q_ref[0, 1 + t * MLA_NCOL + c] for c in range(MLA_NCOL)]

    def store(t, o):
        for g in range(MLA_NCOL):
            o_ref[0, 1 + t * MLA_NCOL + g] = o[:, g * TM:(g + 1) * TM].astype(o_ref.dtype)

    q0 = q_tile(0)
    for c, sl in enumerate(MLA_COLS):
        s_a[:, sl] = _dot(k_ref[0, 0:MLA_TK, :], q0[c])

    def tile_pair(tt, _):
        t = 2 * tt
        store(t, _attend(q_tile(t), q_tile(t + 1), k_ref, vt_ref, s_a, s_b))
        store(t + 1, _attend(q_tile(t + 1), q_tile(t + 2), k_ref, vt_ref, s_b, s_a))
        return 0

    lax.fori_loop(0, n_tiles // 2, tile_pair, 0)


def _mla_call(qm, km, vt):
    assert MLA_TK == TM
    head_t_spec = pl.BlockSpec((1, NT, MLA_V, TM), lambda b, h: (b, 0, h, 0))
    return pl.pallas_call(
        _mla_kernel,
        grid=(B, MLA_H),
        in_specs=[pl.BlockSpec((1, NT, HEAD_SLOT, TM), lambda b, h: (b, 0, h, 0)),
                  pl.BlockSpec((1, TT, HEAD_SLOT), lambda b, h: (b, 0, h)),
                  head_t_spec],
        out_specs=head_t_spec,
        out_shape=jax.ShapeDtypeStruct((B, NT, MLA_H * MLA_V, TM), BF16),
        scratch_shapes=[pltpu.VMEM((MLA_TK, MLA_TQ), F32), pltpu.VMEM((MLA_TK, MLA_TQ), F32)],
        compiler_params=_cparams(("parallel", "arbitrary")),
        name="mla_attn",
    )(qm, km, vt)


def _na_kernel(q_ref, k_ref, vt_ref, tab_ref, o_ref):
    step = pl.program_id(1)
    row_head = jnp.right_shift(lax.broadcasted_iota(jnp.int32, (NA_W, TM), 0), 6)

    def head_queries(g):
        q_all = q_ref[0, g]
        return [jnp.where(row_head == h, q_all, jnp.zeros_like(q_all)) for h in range(NA_H)]

    def values(g, h):
        return vt_ref[0, g, h * NA_DH:(h + 1) * NA_DH, :]

    def attend(g, h, p, vt):
        acc = _dot(_with_ones(vt), p)
        o_ref[0, g, h * NA_DH:(h + 1) * NA_DH, :] = (acc[:NA_DH] / acc[NA_DH:NA_DH + 1]).astype(o_ref.dtype)

    def softmax(s):
        return jnp.exp(s - jnp.max(s, axis=0, keepdims=True)).astype(BF16)

    @pl.when(step == 0)
    def _():
        q_heads = head_queries(0)
        for h in range(NA_H):
            attend(0, h, softmax(_dot(k_ref[0, 0], q_heads[h])), values(0, h))

    @pl.when(step > 0)
    def _():
        items = []
        for u in range(NA_STEP_GROUPS):
            jj = (step - 1) * NA_STEP_GROUPS + u
            g0 = jnp.clip(jj - 1, 0, NA_NGROUPS - NA_BAND_G)
            variant = jnp.where(jj == 0, 0, jnp.where(jj == NA_NGROUPS - 1, 2, 1))
            band = k_ref[0, pl.ds(1 + g0, NA_BAND_G)].reshape(NA_BAND_G * TM, NA_W)
            keys = jnp.concatenate([k_ref[0, 0], band], axis=0)
            q_heads = head_queries(1 + jj)
            items += [(1 + jj, g0, variant, keys, q_heads[h], h) for h in range(NA_H)]
        scores, probs = {}, {}
        for t in range(len(items) + 2):
            if t < len(items):
                scores[t] = _dot(items[t][3], items[t][4])
            if 0 <= t - 1 < len(items):
                _, _, variant, _, _, h = items[t - 1]
                s = scores.pop(t - 1)
                probs[t - 1] = softmax(jnp.concatenate([s[:TM], s[TM:] + tab_ref[variant, h]], axis=0))
            if 0 <= t - 2 < len(items):
                g, g0, _, _, _, h = items[t - 2]
                vt = jnp.concatenate([values(0, h)] + [values(1 + g0 + i, h) for i in range(NA_BAND_G)], axis=1)
                attend(g, h, probs.pop(t - 2), vt)


NA_STEP_GROUPS = 4


def _na_call(q, k, vt, tab, layer):
    assert NA_NGROUPS % NA_STEP_GROUPS == 0
    whole = pl.BlockSpec((1, NT, TM, NA_W), lambda b, j: (b, 0, 0, 0))
    return pl.pallas_call(
        _na_kernel,
        grid=(B, 1 + NA_NGROUPS // NA_STEP_GROUPS),
        in_specs=[whole, whole, whole, _layer_spec((3, NA_H, NA_BAND_G * TM, TM), layer)],
        out_specs=whole,
        out_shape=jax.ShapeDtypeStruct((B, NT, NA_W, TM), BF16),
        compiler_params=_cparams(("parallel", "arbitrary")),
        name="na_attn",
    )(q, k, vt, tab)


def _rope_tables():
    t = jnp.arange(SEQ)
    pos = jnp.stack([t // GRID_W, t % GRID_W], axis=-1).astype(F32)
    half = MLA_ROPE // 2
    inv = ROPE_BASE ** (-jnp.arange(0, half, 2, dtype=F32) / half)
    ang = pos[:, :, None] * inv
    cos, sin = jnp.cos(ang), jnp.sin(ang)
    ones = jnp.ones((SEQ, MLA_NOPE), F32)
    c = jnp.concatenate([ones, cos[:, 0], cos[:, 0], cos[:, 1], cos[:, 1], ones[:, :32]], axis=-1)
    s = jnp.concatenate([0 * ones, -sin[:, 0], sin[:, 0], -sin[:, 1], sin[:, 1], 0 * ones[:, :32]], axis=-1)
    ctx_c = jnp.ones((CTX, LANES), F32)
    ctx_s = jnp.zeros((CTX, LANES), F32)
    return jnp.concatenate([ctx_c, c]), jnp.concatenate([ctx_s, s])


ROPE_PARTNER = np.concatenate([np.arange(8, 16), np.arange(0, 8), np.arange(24, 32), np.arange(16, 24)])


def _group_ones(width, group):
    g = np.arange(width) // group
    return jnp.asarray(g[:, None] == g[None, :], BF16)


def _na_window_pattern():
    n_band = NA_BAND_G * NA_QROWS
    i = np.arange(n_band)[:, None]
    a = np.arange(NA_QROWS)[None, :]

    def pattern(jj):
        g0 = min(max(jj - 1, 0), NA_NGROUPS - NA_BAND_G)
        krow = NA_QROWS * g0 + i
        qrow = NA_QROWS * jj + a
        r0q = np.clip(qrow - NA_WR // 2, 0, GRID_H - NA_WR)
        return (krow >= r0q) & (krow < r0q + NA_WR), krow - qrow + NA_WR - 1

    interior = [pattern(jj) for jj in range(1, NA_NGROUPS - 1)]
    assert all((v == interior[0][0]).all() and (r[v] == interior[0][1][v]).all() for v, r in interior)
    pats = [pattern(0), interior[0], pattern(NA_NGROUPS - 1)]
    valid = np.stack([p[0] for p in pats])
    row_off = np.stack([p[1] for p in pats])
    assert (row_off[valid] >= 0).all() and (row_off[valid] <= 2 * NA_WR - 2).all()
    return valid, row_off


def _na_tables(rpb):
    kc = jnp.arange(GRID_W)[:, None]
    qc = jnp.arange(GRID_W)[None, :]
    win = jnp.clip(qc - NA_WC // 2, 0, GRID_W - NA_WC)
    ok_col = (kc >= win) & (kc < win + NA_WC)
    col_off = kc - qc + NA_WC - 1
    pick_col = ((col_off[None] == jnp.arange(2 * NA_WC - 1)[:, None, None]) & ok_col[None]).astype(F32)
    valid, row_off = _na_window_pattern()
    pick_row = ((row_off[..., None] == np.arange(2 * NA_WR - 1)) & valid[..., None]).astype(np.float32)
    eye_a = jnp.eye(NA_QROWS, dtype=F32)
    pick_col = (eye_a[:, None, None, :, None] * pick_col[None, :, :, None, :]).reshape(
        NA_QROWS, 2 * NA_WC - 1, GRID_W, TM)
    rows = jnp.einsum("vian,dhnj->dvhiaj", pick_row, rpb, precision=lax.Precision.HIGHEST)
    tab = jnp.einsum("dvhiaj,ajkz->dvhikz", rows, pick_col, precision=lax.Precision.HIGHEST)
    keep = (valid[:, None, :, None, :, None] & ok_col[None, None, None, :, None, :]).reshape(
        3, 1, NA_BAND_G * NA_QROWS, GRID_W, TM)
    tab = jnp.where(keep, tab, NEG)
    return tab.reshape(DEPTH, 3, NA_H, NA_BAND_G * TM, TM)


def _block_diag(w):
    w = w.reshape(DEPTH, 2, LRU_W // LANES, 2, LRU_BW, LRU_BW)
    z = jnp.zeros_like(w[:, :, :, 0])
    top = jnp.concatenate([w[:, :, :, 0], z], axis=-1)
    bot = jnp.concatenate([z, w[:, :, :, 1]], axis=-1)
    return jnp.concatenate([top, bot], axis=-2)


def _prepare(w_in, w_out, lru_conv_b, lru_w_a, lru_b_a, lru_w_x, lru_b_x, mla_w_uq, mla_w_ukv,
             mla_q_gain, mla_k_gain, na_q_gain, na_k_gain):
    z = lambda n: jnp.zeros((DEPTH, D, n), F32)
    kr = w_in[..., 1152:1184]
    w_in_ext = jnp.concatenate(
        [w_in[..., :1152], z(MLA_NOPE), kr, kr[..., ROPE_PARTNER], w_in[..., 1184:]], axis=-1).astype(BF16)
    w_uq = mla_w_uq.reshape(DEPTH, MLA_QR, MLA_H, MLA_QK)
    slot_pad = ((0, 0), (0, 0), (0, 0), (0, HEAD_SLOT - MLA_QK))
    zq = jnp.zeros((DEPTH, MLA_QR, MLA_H, MLA_NOPE), F32)
    w_uq_partner = jnp.concatenate([zq, w_uq[..., MLA_NOPE + ROPE_PARTNER]], axis=-1)
    w_uq = jnp.concatenate([jnp.pad(w_uq, slot_pad).reshape(DEPTH, MLA_QR, -1),
                            jnp.pad(w_uq_partner, slot_pad).reshape(DEPTH, MLA_QR, -1)], axis=-1)
    w_ukv = mla_w_ukv.reshape(DEPTH, MLA_KVR, MLA_H, MLA_NOPE + MLA_V)
    w_uk = jnp.pad(w_ukv[..., :MLA_NOPE], ((0, 0), (0, 0), (0, 0), (0, HEAD_SLOT - MLA_NOPE)))
    w_ukv_ext = jnp.concatenate([w_uk.reshape(DEPTH, MLA_KVR, -1),
                                 w_ukv[..., MLA_NOPE:].reshape(DEPTH, MLA_KVR, -1)], axis=-1)

    def pad_gain(g):
        partner = jnp.concatenate([jnp.zeros((DEPTH, MLA_NOPE), F32), g[:, MLA_NOPE + ROPE_PARTNER]], axis=-1)
        both = jnp.stack([g, partner], axis=1)
        return jnp.pad(both, ((0, 0), (0, 0), (0, HEAD_SLOT - MLA_QK)))
    tile_gain = lambda g: jnp.tile(g, (1, NA_H)).reshape(DEPTH, 1, NA_W)
    wa, wx = _block_diag(lru_w_a), _block_diag(lru_w_x)
    wg = jnp.concatenate([wa[:, 0], wx[:, 0], wa[:, 1], wx[:, 1]], axis=-1).astype(BF16)
    ba = lru_b_a.reshape(DEPTH, 2, LRU_W // LANES, 1, LANES)
    bx = lru_b_x.reshape(DEPTH, 2, LRU_W // LANES, 1, LANES)
    bg = jnp.concatenate([ba[:, 0], bx[:, 0], ba[:, 1], bx[:, 1]], axis=-1)
    w_out = w_out.astype(BF16)
    return dict(
        w_in=w_in_ext, w_uq=w_uq.astype(BF16), w_ukv=w_ukv_ext.astype(BF16),
        mq_gain=pad_gain(mla_q_gain), mk_gain=pad_gain(mla_k_gain),
        nq_gain=tile_gain(na_q_gain), nk_gain=tile_gain(na_k_gain),
        lru_wg=wg, lru_bg=bg, conv_b=lru_conv_b.reshape(DEPTH, 1, LRU_W),
        w_out_lru=w_out[:, :LRU_W], w_out_mla=w_out[:, LRU_W:LRU_W + MLA_H * MLA_V],
        w_out_na=w_out[:, LRU_W + MLA_H * MLA_V:],
    )


def kernel(x, c, ctx, c_ctx, w_mod, b_mod, norm_ffn1, ffn1_w_in, ffn1_w_out, norm_mix, w_in, w_out, lru_conv_w, lru_conv_b, lru_w_a, lru_b_a, lru_w_x, lru_b_x, lru_lambda, mla_q_norm, mla_w_uq, mla_kv_norm, mla_w_ukv, mla_q_gain, mla_k_gain, na_q_gain, na_k_gain, na_rpb, norm_ffn2, ffn2_w_in, ffn2_w_out):
    cc = jnp.concatenate([c, c_ctx[None], jnp.zeros((8 - B - 1, D), F32)], axis=0)
    mod = _mod_call(cc, w_mod, b_mod)
    mod = mod.reshape(DEPTH, 8, 3, 3, D)
    mod = jnp.stack([jnp.broadcast_to(mod[:, B:B + 1], (DEPTH, B, 3, 3, D)), mod[:, :B]], axis=2)

    prm = _prepare(w_in, w_out, lru_conv_b, lru_w_a, lru_b_a, lru_w_x, lru_b_x, mla_w_uq, mla_w_ukv,
                   mla_q_gain, mla_k_gain, na_q_gain, na_k_gain)
    prm.update(norm_mix=norm_mix.reshape(DEPTH, 1, D), q_norm=mla_q_norm.reshape(DEPTH, 1, MLA_QR),
               kv_norm=mla_kv_norm.reshape(DEPTH, 1, MLA_KVR), conv_w=lru_conv_w, lru_lam=lru_lambda)
    f1_in, f1_out = ffn1_w_in.astype(BF16), ffn1_w_out.astype(BF16)
    f2_in, f2_out = ffn2_w_in.astype(BF16), ffn2_w_out.astype(BF16)
    g1, g2 = norm_ffn1.reshape(DEPTH, 1, D), norm_ffn2.reshape(DEPTH, 1, D)
    rope_tabs = _rope_tables() + (_group_ones(MXU_TILE, HEAD_SLOT), _group_ones(MXU_TILE, NA_DH))
    na_tabs = _na_tables(na_rpb)

    h = jnp.concatenate([ctx, x], axis=1)
    for l in range(DEPTH):
        p = {k: v[l] for k, v in prm.items()}
        h = _ffn_call(h, mod[l], g1[l], f1_in, f1_out, l, 0, "ffn1")
        lx, lg, qm, km, vm, nq, nk, nv = _mixin_call(h, mod[l], p, rope_tabs)
        ylru = _lru_call(lx, lg, p)
        ymla = _mla_call(qm, km, vm)
        yna = _na_call(nq, nk.reshape(B, NT, TM, NA_W), nv, na_tabs, l)
        h = _post_call(h, mod[l], ylru, ymla, yna, p, g2[l], f2_in, f2_out, l)
    return h[:, CTX:]
```

```python
import math

import jax
import jax.numpy as jnp
import numpy as np
from jax import lax
from jax.experimental import pallas as pl
from jax.experimental.pallas import tpu as pltpu

F32 = jnp.float32
BF16 = jnp.bfloat16

D = 1024
B = 4
SEQ = 8192
DEPTH = 4
GRID_W = 64
GRID_H = SEQ // GRID_W
CTX = 256
TT = CTX + SEQ
D_FF = 2816
EPS = 1e-6

LRU_W = 384
LRU_BW = 64
LRU_C = 8.0
CONV_W = 4
CONV_LEFT = 2

MLA_H = 6
MLA_QR = 256
MLA_KVR = 128
MLA_NOPE = 64
MLA_ROPE = 32
MLA_QK = MLA_NOPE + MLA_ROPE
MLA_V = 64
ROPE_BASE = 10000.0

NA_H = 4
NA_DH = 64
NA_W = NA_H * NA_DH
NA_WR = 8
NA_WC = 16

LANES = 128
SUB = 8
MXU_TILE = 256
HEAD_SLOT = LANES
(COL_LX, COL_LG, COL_CQ, COL_CKV, COL_KR, COL_NQ, COL_NK, COL_NV, IN_EXT) = np.cumsum(
    [0, LRU_W, LRU_W, MLA_QR, MLA_KVR, HEAD_SLOT, NA_W, NA_W, NA_W]).tolist()
TM = 256
NT = TT // TM
V_ROWS = MLA_V + 16
NEG = -1e30
VMEM_LIMIT = 56 * 1024 * 1024

NA_QROWS = 4
NA_NGROUPS = SEQ // (NA_QROWS * GRID_W)
NA_BAND_G = 3

MLA_QSCALE = MLA_QK ** -0.5 * math.log2(math.e)
NA_QSCALE = NA_DH ** -0.5


def _cparams(sem):
    return pltpu.CompilerParams(dimension_semantics=sem, vmem_limit_bytes=VMEM_LIMIT)


def _dot(a, b):
    return jnp.dot(a, b, preferred_element_type=F32)


def _rms(x, n=None):
    n = x.shape[-1] if n is None else n
    return x * lax.rsqrt(jnp.sum(x * x, axis=-1, keepdims=True) * (1.0 / n) + EPS)


def _mod_kernel(c_ref, w_ref, b_ref, o_ref):
    s = jax.nn.silu(c_ref[...]).astype(BF16)
    o_ref[0] = _dot(s, w_ref[0].astype(BF16)) + b_ref[0]


def _mod_call(cc, w_mod, b_mod):
    n = w_mod.shape[-1]
    tn = n // 4
    return pl.pallas_call(
        _mod_kernel,
        grid=(DEPTH, n // tn),
        in_specs=[
            pl.BlockSpec((SUB, D), lambda l, j: (0, 0)),
            pl.BlockSpec((1, D, tn), lambda l, j: (l, 0, j)),
            pl.BlockSpec((1, 1, tn), lambda l, j: (l, 0, j)),
        ],
        out_specs=pl.BlockSpec((1, SUB, tn), lambda l, j: (l, 0, j)),
        out_shape=jax.ShapeDtypeStruct((DEPTH, SUB, n), F32),
        compiler_params=_cparams(("arbitrary", "arbitrary")),
        name="mod",
    )(cc, w_mod, b_mod.reshape(DEPTH, 1, n))


TOK = 768
GROUPS = TOK // TM
NTOK = TT // TOK


def _norm_mod(x, g, mod_ref):
    row = pl.program_id(1) * TOK + lax.broadcasted_iota(jnp.int32, (x.shape[0], 1), 0)
    is_ctx = row < CTX
    mod_c, mod_l = mod_ref[0, 0, 0], mod_ref[0, 1, 0]
    pick = lambda r: jnp.where(is_ctx, mod_c[r:r + 1], mod_l[r:r + 1])
    xn = _rms(x) * g
    return xn * (1.0 + pick(1)) + pick(0), pick(2)


FFN_SPLIT = (D_FF // MXU_TILE + 1) // 2 * MXU_TILE
FFN_CHUNKS = ((0, FFN_SPLIT), (FFN_SPLIT, D_FF))


def _ffn_body(x, mod_ref, g_ref, win_ref, wout_ref, chunks):
    xm, gate_row = _norm_mod(x, g_ref[...], mod_ref)
    xb = xm.astype(BF16)
    acc = jnp.zeros((x.shape[0], D), F32)
    for lo, hi in chunks:
        gate = _dot(xb, win_ref[:, lo:hi])
        up = _dot(xb, win_ref[:, D_FF + lo:D_FF + hi])
        a = (jax.nn.silu(gate) * up).astype(BF16)
        acc = acc + _dot(a, wout_ref[lo:hi, :])
    return x + 0.5 * gate_row * acc


def _ffn_kernel(x_ref, mod_ref, g_ref, win_ref, wout_ref, o_ref):
    o_ref[0] = _ffn_body(x_ref[0], mod_ref, g_ref, win_ref, wout_ref, FFN_CHUNKS)


def _mod_spec(group):
    return pl.BlockSpec((1, 2, 1, 3, D), lambda b, i: (b, 0, group, 0, 0))


def _tok_spec(width):
    return pl.BlockSpec((1, TOK, width), lambda b, i: (b, i, 0))


def _const_spec(shape):
    return pl.BlockSpec(shape, lambda b, i: (0,) * len(shape), pipeline_mode=pl.Buffered(1))


def _layer_spec(shape, layer):
    return pl.BlockSpec((None,) + shape, lambda b, i: (layer,) + (0,) * len(shape),
                        pipeline_mode=pl.Buffered(1))


def _ffn_first_kernel(ctx_ref, *refs):
    x_refs, (mod_ref, g_ref, win_ref, wout_ref, o_ref) = refs[:GROUPS], refs[GROUPS:]
    first = jnp.where(pl.program_id(1) == 0, ctx_ref[0], x_refs[0][0])
    x = jnp.concatenate([first] + [r[0] for r in x_refs[1:]], axis=0)
    o_ref[0] = _ffn_body(x, mod_ref, g_ref, win_ref, wout_ref, FFN_CHUNKS)


def _ffn_first_call(ctx, x, mod, g, w_in, w_out, layer, group):
    assert CTX == TM

    def latent_group(s):
        return pl.BlockSpec((1, TM, D), lambda b, i: (b, jnp.maximum(i * GROUPS + s - 1, 0), 0))

    return pl.pallas_call(
        _ffn_first_kernel,
        grid=(B, NTOK),
        in_specs=[pl.BlockSpec((1, CTX, D), lambda b, i: (b, 0, 0))] + [latent_group(s) for s in range(GROUPS)]
        + [_mod_spec(group), _const_spec((1, D)),
           _layer_spec((D, 2 * D_FF), layer), _layer_spec((D_FF, D), layer)],
        out_specs=_tok_spec(D),
        out_shape=jax.ShapeDtypeStruct((B, TT, D), F32),
        compiler_params=_cparams(("parallel", "arbitrary")),
        name="ffn1_first",
    )(ctx, *([x] * GROUPS), mod, g, w_in, w_out)


def _ffn_call(h, mod, g, w_in, w_out, layer, group, name):
    return pl.pallas_call(
        _ffn_kernel,
        grid=(B, NTOK),
        in_specs=[_tok_spec(D), _mod_spec(group), _const_spec((1, D)),
                  _layer_spec((D, 2 * D_FF), layer), _layer_spec((D_FF, D), layer)],
        out_specs=_tok_spec(D),
        out_shape=jax.ShapeDtypeStruct((B, TT, D), F32),
        compiler_params=_cparams(("parallel", "arbitrary")),
        name=name,
    )(h, mod, g, w_in, w_out)


def _group_ssq(x, ones_ref):
    sq = x * x
    hi = sq.astype(BF16)
    lo = (sq - hi.astype(F32)).astype(BF16)
    ones = ones_ref[...]
    tiles = [slice(c, c + MXU_TILE) for c in range(0, x.shape[1], MXU_TILE)]
    return jnp.concatenate([_dot(hi[:, t], ones) + _dot(lo[:, t], ones) for t in tiles], axis=1)


def _mixin_kernel(x_ref, mod_ref, g_ref, win_ref, qn_ref, wuq_ref, kvn_ref, wukv_ref,
                  mqg_ref, mkg_ref, nqg_ref, nkg_ref, rc_ref, rs_ref, ones_h_ref, ones_n_ref,
                  lx_ref, lg_ref, qm_ref, km_ref, vm_ref, nq_ref, nk_ref, nv_ref):
    x = x_ref[0]
    xb = _norm_mod(x, g_ref[...], mod_ref)[0].astype(BF16)
    y = _dot(xb, win_ref[...])
    lx_ref[0] = y[:, COL_LX:COL_LG]
    lg_ref[0] = y[:, COL_LG:COL_CQ]
    cq = y[:, COL_CQ:COL_CKV]
    ckv = y[:, COL_CKV:COL_KR]
    kr_slot = y[:, COL_KR:COL_NQ]
    rc, rs = rc_ref[...], rs_ref[...]
    n_slot = MLA_H * HEAD_SLOT

    q_raw = _dot((_rms(cq) * qn_ref[...]).astype(BF16), wuq_ref[...])
    inv_q = lax.rsqrt(_group_ssq(q_raw, ones_h_ref) * (1.0 / MLA_QK) + EPS)
    cos_q = rc * (mqg_ref[0:1] * MLA_QSCALE)
    sin_q = rs * (mqg_ref[1:2] * MLA_QSCALE)

    kv = _dot((_rms(ckv) * kvn_ref[...]).astype(BF16), wukv_ref[...])
    lane = lax.broadcasted_iota(jnp.int32, kr_slot.shape, 1)
    kr = jnp.where(lane < MLA_QK, kr_slot, 0.0)
    kr_partner = pltpu.roll(kr_slot, LANES - MLA_ROPE, 1)
    k_raw = kv[:, :n_slot] + jnp.concatenate([kr] * MLA_H, axis=1)
    inv_k = lax.rsqrt(_group_ssq(k_raw, ones_h_ref) * (1.0 / MLA_QK) + EPS)
    cos_k = rc * mkg_ref[0:1]
    sin_k = kr_partner * (rs * mkg_ref[1:2])
    q_heads = []
    for h in range(MLA_H):
        sl = slice(h * HEAD_SLOT, (h + 1) * HEAD_SLOT)
        qh = q_raw[:, sl] * cos_q + pltpu.roll(q_raw[:, sl], LANES - MLA_ROPE, 1) * sin_q
        q_heads.append(qh * inv_q[:, sl])
        km_ref[0, :, sl] = ((k_raw[:, sl] * cos_k + sin_k) * inv_k[:, sl]).astype(BF16)
    q_all = jnp.concatenate(q_heads, axis=1)
    for s in range(GROUPS):
        qm_ref[0, s] = q_all[s * TM:(s + 1) * TM].T.astype(BF16)
        vm_ref[0, s] = kv[s * TM:(s + 1) * TM, n_slot:].T.astype(BF16)

    nq = y[:, COL_NQ:COL_NK]
    nq = nq * lax.rsqrt(_group_ssq(nq, ones_n_ref) * (1.0 / NA_DH) + EPS) * (nqg_ref[...] * NA_QSCALE)
    nk = y[:, COL_NK:COL_NV]
    nk_ref[0] = (nk * lax.rsqrt(_group_ssq(nk, ones_n_ref) * (1.0 / NA_DH) + EPS) * nkg_ref[...]).astype(BF16)
    for s in range(GROUPS):
        nq_ref[0, s] = nq[s * TM:(s + 1) * TM].T.astype(BF16)
        nv_ref[0, s] = y[s * TM:(s + 1) * TM, COL_NV:IN_EXT].T.astype(BF16)


def _tile_t_spec(rows):
    return pl.BlockSpec((1, GROUPS, rows, TM), lambda b, i: (b, i, 0, 0))


def _mixin_call(h, mod, p, tabs):
    n_slot = MLA_H * HEAD_SLOT
    tok = lambda w, dt: (_tok_spec(w), jax.ShapeDtypeStruct((B, TT, w), dt))
    tile_t = lambda rows: (_tile_t_spec(rows), jax.ShapeDtypeStruct((B, NT, rows, TM), BF16))
    outs = [tok(LRU_W, F32), tok(LRU_W, F32), tile_t(n_slot), tok(n_slot, BF16), tile_t(MLA_H * MLA_V),
            tile_t(NA_W), tok(NA_W, BF16), tile_t(NA_W)]
    out_specs, out_shape = [o[0] for o in outs], [o[1] for o in outs]
    rope_spec = pl.BlockSpec((TOK, LANES), lambda b, i: (i, 0))
    return pl.pallas_call(
        _mixin_kernel,
        grid=(B, NTOK),
        in_specs=[_tok_spec(D), _mod_spec(1), _const_spec((1, D)), _const_spec((D, IN_EXT)),
                  _const_spec((1, MLA_QR)), _const_spec((MLA_QR, n_slot)),
                  _const_spec((1, MLA_KVR)), _const_spec((MLA_KVR, n_slot + MLA_H * MLA_V)),
                  _const_spec((2, HEAD_SLOT)), _const_spec((2, HEAD_SLOT)),
                  _const_spec((1, NA_W)), _const_spec((1, NA_W)),
                  rope_spec, rope_spec, _const_spec((MXU_TILE, MXU_TILE)), _const_spec((MXU_TILE, MXU_TILE))],
        out_specs=out_specs,
        out_shape=out_shape,
        compiler_params=_cparams(("parallel", "arbitrary")),
        name="mix_in",
    )(h, mod, p["norm_mix"], p["w_in"], p["q_norm"], p["w_uq"], p["kv_norm"], p["w_ukv"],
      p["mq_gain"], p["mk_gain"], p["nq_gain"], p["nk_gain"], *tabs)


def _dot_tn(a, b):
    return lax.dot_general(a, b, (((0,), (0,)), ((), ())), preferred_element_type=F32)


def _ffn_chunks(tiles):
    return tuple((lo, min(lo + tiles * MXU_TILE, D_FF)) for lo in range(0, D_FF, tiles * MXU_TILE))


POST_CHUNKS = _ffn_chunks(3)


def _post_kernel(h_ref, modm_ref, ylru_ref, ymla_ref, yna_ref, w1_ref, w2_ref, w3_ref,
                 modf_ref, g_ref, win_ref, wout_ref, o_ref):
    att = jnp.concatenate([_dot_tn(ymla_ref[0, s], w2_ref[...]) + _dot_tn(yna_ref[0, s], w3_ref[...])
                           for s in range(GROUPS)], axis=0)
    y = _dot(ylru_ref[0], w1_ref[...]) + att
    row = pl.program_id(1) * TOK + lax.broadcasted_iota(jnp.int32, (TOK, 1), 0)
    gate = jnp.where(row < CTX, modm_ref[0, 0, 0][2:3], modm_ref[0, 1, 0][2:3])
    h = h_ref[0] + gate * y
    o_ref[0] = _ffn_body(h, modf_ref, g_ref, win_ref, wout_ref, POST_CHUNKS)


def _post_call(h, mod, ylru, ymla, yna, p, g, w_in, w_out, layer):
    return pl.pallas_call(
        _post_kernel,
        grid=(B, NTOK),
        in_specs=[_tok_spec(D), _mod_spec(1), _tok_spec(LRU_W), _tile_t_spec(MLA_H * MLA_V), _tile_t_spec(NA_W),
                  _const_spec((LRU_W, D)), _const_spec((MLA_H * MLA_V, D)), _const_spec((NA_W, D)),
                  _mod_spec(2), _const_spec((1, D)),
                  _layer_spec((D, 2 * D_FF), layer), _layer_spec((D_FF, D), layer)],
        out_specs=_tok_spec(D),
        out_shape=jax.ShapeDtypeStruct((B, TT, D), F32),
        compiler_params=_cparams(("parallel", "arbitrary")),
        name="post",
    )(h, mod, ylru, ymla, yna, p["w_out_lru"], p["w_out_mla"], p["w_out_na"], mod, g, w_in, w_out)


LRU_CHUNK = TM
LRU_PAD = SUB
LRU_XROWS = TT + 3 * LRU_PAD


def _scan_block(a, u, carry, reverse):
    row = lax.broadcasted_iota(jnp.int32, a.shape, 0)
    for s in (1, 2, 4):
        if reverse:
            keep = row < SUB - s
            a_s = jnp.where(keep, pltpu.roll(a, SUB - s, 0), 1.0)
            u_s = jnp.where(keep, pltpu.roll(u, SUB - s, 0), 0.0)
        else:
            keep = row >= s
            a_s = jnp.where(keep, pltpu.roll(a, s, 0), 1.0)
            u_s = jnp.where(keep, pltpu.roll(u, s, 0), 0.0)
        u = a * u_s + u
        a = a * a_s
    h = a * carry + u
    edge = h[0:1] if reverse else h[SUB - 1:SUB]
    return h, jnp.broadcast_to(edge, a.shape)


LRU_SLICE = 64
LRU_NCHUNK = TT // LRU_CHUNK


def _lru_kernel(lx_ref, lg_ref, cw_ref, cb_ref, wg_ref, bg_ref, lam_ref, y_ref, xpad, hf, hb, au0, au1):
    zeros_pad = jnp.zeros((LRU_PAD, LANES), F32)
    xpad[0:LRU_PAD] = zeros_pad
    xpad[LRU_PAD:LRU_PAD + CTX] = lx_ref[0, 0:CTX]
    xpad[LRU_PAD + CTX:2 * LRU_PAD + CTX] = zeros_pad
    xpad[2 * LRU_PAD + CTX:2 * LRU_PAD + TT] = lx_ref[0, CTX:TT]
    xpad[2 * LRU_PAD + TT:3 * LRU_PAD + TT] = zeros_pad

    neg_c_softplus = -LRU_C * jax.nn.softplus(-lam_ref[...])
    n_slices = LRU_CHUNK // LRU_SLICE
    blocks_per_slice = LRU_SLICE // SUB

    def fwd_chunk(i):
        return i

    def bwd_chunk(i):
        return jnp.where(i == 0, 0, LRU_NCHUNK - i)

    def gates(c, d, s, buf):
        base = pl.multiple_of(jnp.where(c == 0, LRU_PAD, 2 * LRU_PAD + c * LRU_CHUNK), SUB) + s * LRU_SLICE
        xc = jnp.broadcast_to(cb_ref[...], (LRU_SLICE, LANES))
        for j in range(CONV_W):
            xc = xc + xpad[pl.ds(base + (j - CONV_LEFT), LRU_SLICE), :] * cw_ref[j:j + 1, :]
        cols = slice(d * 2 * LANES, (d + 1) * 2 * LANES)
        pre = _dot(xc.astype(BF16), wg_ref[0, :, cols]) + bg_ref[0, :, cols]
        r = jax.nn.sigmoid(pre[:, :LANES])
        gate_in = jax.nn.sigmoid(pre[:, LANES:])
        log_a = r * neg_c_softplus[d:d + 1]
        rows = slice(s * LRU_SLICE, (s + 1) * LRU_SLICE)
        buf[2 * d, rows, :] = jnp.exp(log_a)
        th = jnp.tanh(log_a)
        buf[2 * d + 1, rows, :] = jnp.sqrt(-2.0 * th / (1.0 - th)) * (gate_in * xc)

    def scan(c, d, s, buf, carry, dst):
        row0 = pl.multiple_of(c * LRU_CHUNK, SUB)
        for k in range(blocks_per_slice):
            blk = s * blocks_per_slice + k
            if d == 1:
                blk = LRU_CHUNK // SUB - 1 - blk
            rows = slice(blk * SUB, (blk + 1) * SUB)
            h, carry = _scan_block(buf[2 * d, rows, :], buf[2 * d + 1, rows, :], carry, d == 1)
            dst[pl.ds(row0 + blk * SUB, SUB), :] = h
        return carry

    def step(i, cur, nxt, carry):
        cf, cb = carry
        for s in range(n_slices):
            if nxt is not None:
                gates(fwd_chunk(i + 1), 0, s, nxt)
                gates(bwd_chunk(i + 1), 1, s, nxt)
            cf = scan(fwd_chunk(i), 0, s, cur, cf, hf)
            cb = scan(bwd_chunk(i), 1, s, cur, cb, hb)
        return cf, cb

    for s in range(n_slices):
        gates(0, 0, s, au0)
        gates(0, 1, s, au0)

    def pair(j, carry):
        carry = step(2 * j, au0, au1, carry)
        return step(2 * j + 1, au1, au0, carry)

    assert LRU_NCHUNK % 2 == 1
    zero = jnp.zeros((SUB, LANES), F32)
    carry = lax.fori_loop(0, LRU_NCHUNK // 2, pair, (zero, zero))
    step(LRU_NCHUNK - 1, au0, None, carry)

    def finish(c, _):
        rows = pl.ds(pl.multiple_of(c * LRU_CHUNK, LRU_CHUNK), LRU_CHUNK)
        y_ref[0, rows, :] = ((hf[rows, :] + hb[rows, :]) * jax.nn.gelu(lg_ref[0, rows, :])).astype(y_ref.dtype)
        return 0

    lax.fori_loop(0, LRU_NCHUNK, finish, 0)


def _lru_call(lx, lg, p):
    nl = LRU_W // LANES
    seq_spec = pl.BlockSpec((1, TT, LANES), lambda b, j: (b, 0, j))
    return pl.pallas_call(
        _lru_kernel,
        grid=(B, nl),
        in_specs=[seq_spec, seq_spec,
                  pl.BlockSpec((CONV_W, LANES), lambda b, j: (0, j)),
                  pl.BlockSpec((1, LANES), lambda b, j: (0, j)),
                  pl.BlockSpec((1, LANES, 4 * LANES), lambda b, j: (j, 0, 0)),
                  pl.BlockSpec((1, 1, 4 * LANES), lambda b, j: (j, 0, 0)),
                  pl.BlockSpec((2, LANES), lambda b, j: (0, j))],
        out_specs=seq_spec,
        out_shape=jax.ShapeDtypeStruct((B, TT, LRU_W), BF16),
        scratch_shapes=[pltpu.VMEM((LRU_XROWS, LANES), F32), pltpu.VMEM((TT, LANES), F32),
                        pltpu.VMEM((TT, LANES), F32),
                        pltpu.VMEM((4, LRU_CHUNK, LANES), F32), pltpu.VMEM((4, LRU_CHUNK, LANES), F32)],
        compiler_params=_cparams(("parallel", "arbitrary")),
        name="lru",
    )(lx, lg, p["conv_w"], p["conv_b"], p["lru_wg"], p["lru_bg"], p["lru_lam"])


MLA_TQ = 2048
MLA_TK = TM
MLA_NKV = TT // MLA_TK
MLA_UNROLL = 8
MLA_QCOL = MXU_TILE
MLA_PV_LAG = 1


def _with_ones(vt):
    return jnp.concatenate([vt, jnp.ones((V_ROWS - vt.shape[0], vt.shape[1]), vt.dtype)], axis=0)


def _softmax_pv(s, vt, m, acc):
    m_new = jnp.maximum(m, jnp.max(s, axis=0, keepdims=True))
    p = jnp.exp2(s - m_new).astype(BF16)
    return m_new, acc * jnp.exp2(m - m_new) + _dot(vt, p)


MLA_NCOL = MLA_TQ // MLA_QCOL
MLA_COLS = [slice(c * MLA_QCOL, (c + 1) * MLA_QCOL) for c in range(MLA_NCOL)]


def _attend(q_cols, q_next_cols, k_ref, vt_ref, cur, nxt):
    tk = MLA_TK
    assert MLA_NKV % MLA_UNROLL == 1 and MLA_UNROLL % 2 == 0

    def keys(i):
        return k_ref[0, pl.ds(pl.multiple_of(i * tk, tk), tk), :]

    def values(i):
        return _with_ones(vt_ref[0, i])

    def group(j, carry):
        ms, accs = list(carry[0]), list(carry[1])
        a, b = cur, nxt
        pending = []

        def flush(keep):
            while len(pending) > keep:
                pc, pp, pa, pv = pending.pop(0)
                accs[pc] = accs[pc] * pa + _dot(pv, pp)

        for u in range(MLA_UNROLL):
            i = MLA_UNROLL * j + u
            k_next, v_cur = keys(i + 1), values(i)
            for c, sl in enumerate(MLA_COLS):
                b[:, sl] = _dot(k_next, q_cols[c])
                s = a[:, sl]
                m_new = jnp.maximum(ms[c], jnp.max(s, axis=0, keepdims=True))
                pending.append((c, jnp.exp2(s - m_new).astype(BF16), jnp.exp2(ms[c] - m_new), v_cur))
                ms[c] = m_new
                flush(MLA_PV_LAG)
            a, b = b, a
        flush(0)
        return tuple(ms), tuple(accs)

    init = (tuple(jnp.full((1, MLA_QCOL), NEG, F32) for _ in MLA_COLS),
            tuple(jnp.zeros((V_ROWS, MLA_QCOL), F32) for _ in MLA_COLS))
    ms, accs = lax.fori_loop(0, MLA_NKV // MLA_UNROLL, group, init)
    k_first, v_last = k_ref[0, 0:tk, :], values(MLA_NKV - 1)
    outs = []
    for c, sl in enumerate(MLA_COLS):
        nxt[:, sl] = _dot(k_first, q_next_cols[c])
        _, acc = _softmax_pv(cur[:, sl], v_last, ms[c], accs[c])
        outs.append(acc[:MLA_V] / acc[MLA_V:MLA_V + 1])
    return jnp.concatenate(outs, axis=1)


def _mla_kernel(q_ref, k_ref, vt_ref, o_ref, s_a, s_b):
    sc = _dot(k_ref[0, 0:CTX, :], q_ref[0, 0])
    _, acc = _softmax_pv(sc, _with_ones(vt_ref[0, 0]), jnp.full((1, CTX), NEG, F32),
                         jnp.zeros((V_ROWS, CTX), F32))
    o_ref[0, 0] = (acc[:MLA_V] / acc[MLA_V:MLA_V + 1]).astype(o_ref.dtype)

    n_tiles = SEQ // MLA_TQ
    assert n_tiles % 2 == 0 and MLA_QCOL == TM

    def q_tile(t):
        t = jnp.minimum(t, n_tiles - 1)
        return [q_ref[0, 1 + t * MLA_NCOL + c] for c in range(MLA_NCOL)]

    def store(t, o):
        for g in range(MLA_NCOL):
            o_ref[0, 1 + t * MLA_NCOL + g] = o[:, g * TM:(g + 1) * TM].astype(o_ref.dtype)

    q0 = q_tile(0)
    for c, sl in enumerate(MLA_COLS):
        s_a[:, sl] = _dot(k_ref[0, 0:MLA_TK, :], q0[c])

    def tile_pair(tt, _):
        t = 2 * tt
        store(t, _attend(q_tile(t), q_tile(t + 1), k_ref, vt_ref, s_a, s_b))
        store(t + 1, _attend(q_tile(t + 1), q_tile(t + 2), k_ref, vt_ref, s_b, s_a))
        return 0

    lax.fori_loop(0, n_tiles // 2, tile_pair, 0)


def _mla_call(qm, km, vt):
    assert MLA_TK == TM
    head_t_spec = pl.BlockSpec((1, NT, MLA_V, TM), lambda b, h: (b, 0, h, 0))
    return pl.pallas_call(
        _mla_kernel,
        grid=(B, MLA_H),
        in_specs=[pl.BlockSpec((1, NT, HEAD_SLOT, TM), lambda b, h: (b, 0, h, 0)),
                  pl.BlockSpec((1, TT, HEAD_SLOT), lambda b, h: (b, 0, h)),
                  head_t_spec],
        out_specs=head_t_spec,
        out_shape=jax.ShapeDtypeStruct((B, NT, MLA_H * MLA_V, TM), BF16),
        scratch_shapes=[pltpu.VMEM((MLA_TK, MLA_TQ), F32), pltpu.VMEM((MLA_TK, MLA_TQ), F32)],
        compiler_params=_cparams(("parallel", "arbitrary")),
        name="mla_attn",
    )(qm, km, vt)


def _na_kernel(q_ref, k_ref, vt_ref, tab_ref, o_ref):
    step = pl.program_id(1)
    row_head = lax.broadcasted_iota(jnp.int32, (NA_W, TM), 0) // NA_DH

    def head_queries(g):
        q_all = q_ref[0, g]
        return [jnp.where(row_head == h, q_all, jnp.zeros_like(q_all)) for h in range(NA_H)]

    def values(g, h):
        return vt_ref[0, g, h * NA_DH:(h + 1) * NA_DH, :]

    def attend(g, h, p, vt):
        acc = _dot(_with_ones(vt), p)
        o_ref[0, g, h * NA_DH:(h + 1) * NA_DH, :] = (acc[:NA_DH] / acc[NA_DH:NA_DH + 1]).astype(o_ref.dtype)

    def softmax(s):
        return jnp.exp(s - jnp.max(s, axis=0, keepdims=True)).astype(BF16)

    @pl.when(step == 0)
    def _():
        q_heads = head_queries(0)
        for h in range(NA_H):
            attend(0, h, softmax(_dot(k_ref[0, 0], q_heads[h])), values(0, h))

    @pl.when(step > 0)
    def _():
        items = []
        for u in range(NA_STEP_GROUPS):
            jj = (step - 1) * NA_STEP_GROUPS + u
            g0 = jnp.clip(jj - 1, 0, NA_NGROUPS - NA_BAND_G)
            variant = jnp.where(jj == 0, 0, jnp.where(jj == NA_NGROUPS - 1, 2, 1))
            band = k_ref[0, pl.ds(1 + g0, NA_BAND_G)].reshape(NA_BAND_G * TM, NA_W)
            keys = jnp.concatenate([k_ref[0, 0], band], axis=0)
            q_heads = head_queries(1 + jj)
            items += [(1 + jj, g0, variant, keys, q_heads[h], h) for h in range(NA_H)]
        scores, probs = {}, {}
        for t in range(len(items) + 2):
            if t < len(items):
                scores[t] = _dot(items[t][3], items[t][4])
            if 0 <= t - 1 < len(items):
                _, _, variant, _, _, h = items[t - 1]
                s = scores.pop(t - 1)
                probs[t - 1] = softmax(jnp.concatenate([s[:TM], s[TM:] + tab_ref[variant, h]], axis=0))
            if 0 <= t - 2 < len(items):
                g, g0, _, _, _, h = items[t - 2]
                vt = jnp.concatenate([values(0, h)] + [values(1 + g0 + i, h) for i in range(NA_BAND_G)], axis=1)
                attend(g, h, probs.pop(t - 2), vt)


NA_STEP_GROUPS = 4


def _na_call(q, k, vt, tab, layer):
    assert NA_NGROUPS % NA_STEP_GROUPS == 0
    whole = pl.BlockSpec((1, NT, TM, NA_W), lambda b, j: (b, 0, 0, 0))
    return pl.pallas_call(
        _na_kernel,
        grid=(B, 1 + NA_NGROUPS // NA_STEP_GROUPS),
        in_specs=[whole, whole, whole, _layer_spec((3, NA_H, NA_BAND_G * TM, TM), layer)],
        out_specs=whole,
        out_shape=jax.ShapeDtypeStruct((B, NT, NA_W, TM), BF16),
        compiler_params=_cparams(("parallel", "arbitrary")),
        name="na_attn",
    )(q, k, vt, tab)


def _rope_tables():
    t = jnp.arange(SEQ)
    pos = jnp.stack([t // GRID_W, t % GRID_W], axis=-1).astype(F32)
    half = MLA_ROPE // 2
    inv = ROPE_BASE ** (-jnp.arange(0, half, 2, dtype=F32) / half)
    ang = pos[:, :, None] * inv
    cos, sin = jnp.cos(ang), jnp.sin(ang)
    ones = jnp.ones((SEQ, MLA_NOPE), F32)
    c = jnp.concatenate([ones, cos[:, 0], cos[:, 0], cos[:, 1], cos[:, 1], ones[:, :32]], axis=-1)
    s = jnp.concatenate([0 * ones, -sin[:, 0], sin[:, 0], -sin[:, 1], sin[:, 1], 0 * ones[:, :32]], axis=-1)
    ctx_c = jnp.ones((CTX, LANES), F32)
    ctx_s = jnp.zeros((CTX, LANES), F32)
    return jnp.concatenate([ctx_c, c]), jnp.concatenate([ctx_s, s])


ROPE_PARTNER = np.concatenate([np.arange(8, 16), np.arange(0, 8), np.arange(24, 32), np.arange(16, 24)])


def _group_ones(width, group, used):
    lane = np.arange(width)
    g = lane // group
    return jnp.asarray((g[:, None] == g[None, :]) & (lane[:, None] % group < used), BF16)


def _na_window_pattern():
    n_band = NA_BAND_G * NA_QROWS
    i = np.arange(n_band)[:, None]
    a = np.arange(NA_QROWS)[None, :]

    def pattern(jj):
        g0 = min(max(jj - 1, 0), NA_NGROUPS - NA_BAND_G)
        krow = NA_QROWS * g0 + i
        qrow = NA_QROWS * jj + a
        r0q = np.clip(qrow - NA_WR // 2, 0, GRID_H - NA_WR)
        return (krow >= r0q) & (krow < r0q + NA_WR), krow - qrow + NA_WR - 1

    interior = [pattern(jj) for jj in range(1, NA_NGROUPS - 1)]
    assert all((v == interior[0][0]).all() and (r[v] == interior[0][1][v]).all() for v, r in interior)
    pats = [pattern(0), interior[0], pattern(NA_NGROUPS - 1)]
    valid = np.stack([p[0] for p in pats])
    row_off = np.stack([p[1] for p in pats])
    assert (row_off[valid] >= 0).all() and (row_off[valid] <= 2 * NA_WR - 2).all()
    return valid, row_off


def _na_tables(rpb):
    kc = jnp.arange(GRID_W)[:, None]
    qc = jnp.arange(GRID_W)[None, :]
    win = jnp.clip(qc - NA_WC // 2, 0, GRID_W - NA_WC)
    ok_col = (kc >= win) & (kc < win + NA_WC)
    col_off = kc - qc + NA_WC - 1
    pick_col = ((col_off[None] == jnp.arange(2 * NA_WC - 1)[:, None, None]) & ok_col[None]).astype(F32)
    valid, row_off = _na_window_pattern()
    pick_row = ((row_off[..., None] == np.arange(2 * NA_WR - 1)) & valid[..., None]).astype(np.float32)
    eye_a = jnp.eye(NA_QROWS, dtype=F32)
    pick_col = (eye_a[:, None, None, :, None] * pick_col[None, :, :, None, :]).reshape(
        NA_QROWS, 2 * NA_WC - 1, GRID_W, TM)
    rows = jnp.einsum("vian,dhnj->dvhiaj", pick_row, rpb, precision=lax.Precision.HIGHEST)
    tab = jnp.einsum("dvhiaj,ajkz->dvhikz", rows, pick_col, precision=lax.Precision.HIGHEST)
    keep = (valid[:, None, :, None, :, None] & ok_col[None, None, None, :, None, :]).reshape(
        3, 1, NA_BAND_G * NA_QROWS, GRID_W, TM)
    tab = jnp.where(keep, tab, NEG)
    return tab.reshape(DEPTH, 3, NA_H, NA_BAND_G * TM, TM)


def _block_diag(w):
    w = w.reshape(DEPTH, 2, LRU_W // LANES, 2, LRU_BW, LRU_BW)
    z = jnp.zeros_like(w[:, :, :, 0])
    top = jnp.concatenate([w[:, :, :, 0], z], axis=-1)
    bot = jnp.concatenate([z, w[:, :, :, 1]], axis=-1)
    return jnp.concatenate([top, bot], axis=-2)


def _prepare(w_in, w_out, lru_conv_b, lru_w_a, lru_b_a, lru_w_x, lru_b_x, mla_w_uq, mla_w_ukv,
             mla_q_gain, mla_k_gain, na_q_gain, na_k_gain):
    z = lambda n: jnp.zeros((DEPTH, D, n), F32)
    kr = w_in[..., COL_KR:COL_KR + MLA_ROPE]
    w_in_ext = jnp.concatenate([w_in[..., :COL_KR], z(MLA_NOPE), kr, kr[..., ROPE_PARTNER],
                                w_in[..., COL_KR + MLA_ROPE:]], axis=-1).astype(BF16)
    assert w_in_ext.shape[-1] == IN_EXT
    w_uq = mla_w_uq.reshape(DEPTH, MLA_QR, MLA_H, MLA_QK)
    w_uq = jnp.concatenate([w_uq, w_uq[..., MLA_NOPE + ROPE_PARTNER]], axis=-1).reshape(DEPTH, MLA_QR, -1)
    w_ukv = mla_w_ukv.reshape(DEPTH, MLA_KVR, MLA_H, MLA_NOPE + MLA_V)
    w_uk = jnp.pad(w_ukv[..., :MLA_NOPE], ((0, 0), (0, 0), (0, 0), (0, HEAD_SLOT - MLA_NOPE)))
    w_ukv_ext = jnp.concatenate([w_uk.reshape(DEPTH, MLA_KVR, -1),
                                 w_ukv[..., MLA_NOPE:].reshape(DEPTH, MLA_KVR, -1)], axis=-1)

    def pad_gain(g):
        partner = jnp.concatenate([jnp.zeros((DEPTH, MLA_NOPE), F32), g[:, MLA_NOPE + ROPE_PARTNER]], axis=-1)
        both = jnp.stack([g, partner], axis=1)
        return jnp.pad(both, ((0, 0), (0, 0), (0, HEAD_SLOT - MLA_QK)))
    tile_gain = lambda g: jnp.tile(g, (1, NA_H)).reshape(DEPTH, 1, NA_W)
    wa, wx = _block_diag(lru_w_a), _block_diag(lru_w_x)
    wg = jnp.concatenate([wa[:, 0], wx[:, 0], wa[:, 1], wx[:, 1]], axis=-1).astype(BF16)
    ba = lru_b_a.reshape(DEPTH, 2, LRU_W // LANES, 1, LANES)
    bx = lru_b_x.reshape(DEPTH, 2, LRU_W // LANES, 1, LANES)
    bg = jnp.concatenate([ba[:, 0], bx[:, 0], ba[:, 1], bx[:, 1]], axis=-1)
    w_out = w_out.astype(BF16)
    return dict(
        w_in=w_in_ext, w_uq=w_uq.astype(BF16), w_ukv=w_ukv_ext.astype(BF16),
        mq_gain=pad_gain(mla_q_gain), mk_gain=pad_gain(mla_k_gain),
        nq_gain=tile_gain(na_q_gain), nk_gain=tile_gain(na_k_gain),
        lru_wg=wg, lru_bg=bg, conv_b=lru_conv_b.reshape(DEPTH, 1, LRU_W),
        w_out_lru=w_out[:, :LRU_W], w_out_mla=w_out[:, LRU_W:LRU_W + MLA_H * MLA_V],
        w_out_na=w_out[:, LRU_W + MLA_H * MLA_V:],
    )


def kernel(x, c, ctx, c_ctx, w_mod, b_mod, norm_ffn1, ffn1_w_in, ffn1_w_out, norm_mix, w_in, w_out, lru_conv_w, lru_conv_b, lru_w_a, lru_b_a, lru_w_x, lru_b_x, lru_lambda, mla_q_norm, mla_w_uq, mla_kv_norm, mla_w_ukv, mla_q_gain, mla_k_gain, na_q_gain, na_k_gain, na_rpb, norm_ffn2, ffn2_w_in, ffn2_w_out):
    cc = jnp.concatenate([c, c_ctx[None], jnp.zeros((SUB - B - 1, D), F32)], axis=0)
    mod = _mod_call(cc, w_mod, b_mod)
    mod = mod.reshape(DEPTH, SUB, 3, 3, D)
    mod = jnp.stack([jnp.broadcast_to(mod[:, B:B + 1], (DEPTH, B, 3, 3, D)), mod[:, :B]], axis=2)

    prm = _prepare(w_in, w_out, lru_conv_b, lru_w_a, lru_b_a, lru_w_x, lru_b_x, mla_w_uq, mla_w_ukv,
                   mla_q_gain, mla_k_gain, na_q_gain, na_k_gain)
    prm.update(norm_mix=norm_mix.reshape(DEPTH, 1, D), q_norm=mla_q_norm.reshape(DEPTH, 1, MLA_QR),
               kv_norm=mla_kv_norm.reshape(DEPTH, 1, MLA_KVR), conv_w=lru_conv_w, lru_lam=lru_lambda)
    f1_in, f1_out = ffn1_w_in.astype(BF16), ffn1_w_out.astype(BF16)
    f2_in, f2_out = ffn2_w_in.astype(BF16), ffn2_w_out.astype(BF16)
    g1, g2 = norm_ffn1.reshape(DEPTH, 1, D), norm_ffn2.reshape(DEPTH, 1, D)
    rope_tabs = _rope_tables() + (_group_ones(MXU_TILE, HEAD_SLOT, MLA_QK), _group_ones(MXU_TILE, NA_DH, NA_DH))
    na_tabs = _na_tables(na_rpb)

    h = None
    for l in range(DEPTH):
        p = {k: v[l] for k, v in prm.items()}
        if l == 0:
            h = _ffn_first_call(ctx, x, mod[l], g1[l], f1_in, f1_out, l, 0)
        else:
            h = _ffn_call(h, mod[l], g1[l], f1_in, f1_out, l, 0, "ffn1")
        lx, lg, qm, km, vm, nq, nk, nv = _mixin_call(h, mod[l], p, rope_tabs)
        ylru = _lru_call(lx, lg, p)
        ymla = _mla_call(qm, km, vm)
        yna = _na_call(nq, nk.reshape(B, NT, TM, NA_W), nv, na_tabs, l)
        h = _post_call(h, mod[l], ylru, ymla, yna, p, g2[l], f2_in, f2_out, l)
    return h[:, CTX:]
```

```python
import math

import jax
import jax.numpy as jnp
import numpy as np
from jax import lax
from jax.experimental import pallas as pl
from jax.experimental.pallas import tpu as pltpu

F32 = jnp.float32
BF16 = jnp.bfloat16

D = 1024
B = 4
SEQ = 8192
DEPTH = 4
GRID_W = 64
GRID_H = SEQ // GRID_W
CTX = 256
TT = CTX + SEQ
D_FF = 2816
EPS = 1e-6

LRU_W = 384
LRU_BW = 64
LRU_C = 8.0
CONV_W = 4
CONV_LEFT = 2

MLA_H = 6
MLA_QR = 256
MLA_KVR = 128
MLA_NOPE = 64
MLA_ROPE = 32
MLA_QK = MLA_NOPE + MLA_ROPE
MLA_V = 64
ROPE_BASE = 10000.0

NA_H = 4
NA_DH = 64
NA_W = NA_H * NA_DH
NA_WR = 8
NA_WC = 16

LANES = 128
SUB = 8
MXU_TILE = 256
HEAD_SLOT = LANES
(COL_LX, COL_LG, COL_CQ, COL_CKV, COL_KR, COL_NQ, COL_NK, COL_NV, IN_EXT) = np.cumsum(
    [0, LRU_W, LRU_W, MLA_QR, MLA_KVR, HEAD_SLOT, NA_W, NA_W, NA_W]).tolist()
TM = 256
NT = TT // TM
V_ROWS = MLA_V + 16
NEG = -1e30
VMEM_LIMIT = 56 * 1024 * 1024

NA_QROWS = 4
NA_NGROUPS = SEQ // (NA_QROWS * GRID_W)
NA_BAND_G = 3

MLA_QSCALE = MLA_QK ** -0.5 * math.log2(math.e)
NA_QSCALE = NA_DH ** -0.5 * math.log2(math.e)


def _cparams(sem):
    return pltpu.CompilerParams(dimension_semantics=sem, vmem_limit_bytes=VMEM_LIMIT)


def _dot(a, b):
    return jnp.dot(a, b, preferred_element_type=F32)


def _rms(x, n=None):
    n = x.shape[-1] if n is None else n
    return x * lax.rsqrt(jnp.sum(x * x, axis=-1, keepdims=True) * (1.0 / n) + EPS)


def _mod_kernel(c_ref, w_ref, b_ref, o_ref):
    s = jax.nn.silu(c_ref[...]).astype(BF16)
    o_ref[0] = _dot(s, w_ref[0].astype(BF16)) + b_ref[0]


def _mod_call(cc, w_mod, b_mod):
    n = w_mod.shape[-1]
    tn = n // 4
    return pl.pallas_call(
        _mod_kernel,
        grid=(DEPTH, n // tn),
        in_specs=[
            pl.BlockSpec((SUB, D), lambda l, j: (0, 0)),
            pl.BlockSpec((1, D, tn), lambda l, j: (l, 0, j)),
            pl.BlockSpec((1, 1, tn), lambda l, j: (l, 0, j)),
        ],
        out_specs=pl.BlockSpec((1, SUB, tn), lambda l, j: (l, 0, j)),
        out_shape=jax.ShapeDtypeStruct((DEPTH, SUB, n), F32),
        compiler_params=_cparams(("arbitrary", "arbitrary")),
        name="mod",
    )(cc, w_mod, b_mod.reshape(DEPTH, 1, n))


TOK = 768
GROUPS = TOK // TM
NTOK = TT // TOK


def _norm_mod(x, g, mod_ref):
    row = pl.program_id(1) * TOK + lax.broadcasted_iota(jnp.int32, (x.shape[0], 1), 0)
    is_ctx = row < CTX
    mod_c, mod_l = mod_ref[0, 0, 0], mod_ref[0, 1, 0]
    pick = lambda r: jnp.where(is_ctx, mod_c[r:r + 1], mod_l[r:r + 1])
    xn = _rms(x) * g
    return xn * (1.0 + pick(1)) + pick(0), pick(2)


FFN_SPLIT = (D_FF // MXU_TILE + 1) // 2 * MXU_TILE
FFN_CHUNKS = ((0, FFN_SPLIT), (FFN_SPLIT, D_FF))


def _ffn_body(x, mod_ref, g_ref, win_ref, wout_ref, chunks):
    xm, gate_row = _norm_mod(x, g_ref[...], mod_ref)
    xb = xm.astype(BF16)
    acc = jnp.zeros((x.shape[0], D), F32)
    for lo, hi in chunks:
        gate = _dot(xb, win_ref[:, lo:hi])
        up = _dot(xb, win_ref[:, D_FF + lo:D_FF + hi])
        a = (jax.nn.silu(gate) * up).astype(BF16)
        acc = acc + _dot(a, wout_ref[lo:hi, :])
    return x + 0.5 * gate_row * acc


def _ffn_kernel(x_ref, mod_ref, g_ref, win_ref, wout_ref, o_ref):
    o_ref[0] = _ffn_body(x_ref[0], mod_ref, g_ref, win_ref, wout_ref, FFN_CHUNKS)


def _mod_spec(group):
    return pl.BlockSpec((1, 2, 1, 3, D), lambda b, i: (b, 0, group, 0, 0))


def _tok_spec(width):
    return pl.BlockSpec((1, TOK, width), lambda b, i: (b, i, 0))


def _const_spec(shape):
    return pl.BlockSpec(shape, lambda b, i: (0,) * len(shape), pipeline_mode=pl.Buffered(1))


def _layer_spec(shape, layer):
    return pl.BlockSpec((None,) + shape, lambda b, i: (layer,) + (0,) * len(shape),
                        pipeline_mode=pl.Buffered(1))


def _ffn_first_kernel(ctx_ref, *refs):
    x_refs, (mod_ref, g_ref, win_ref, wout_ref, o_ref) = refs[:GROUPS], refs[GROUPS:]
    first = jnp.where(pl.program_id(1) == 0, ctx_ref[0], x_refs[0][0])
    x = jnp.concatenate([first] + [r[0] for r in x_refs[1:]], axis=0)
    o_ref[0] = _ffn_body(x, mod_ref, g_ref, win_ref, wout_ref, FFN_CHUNKS)


def _ffn_first_call(ctx, x, mod, g, w_in, w_out, layer, group):
    assert CTX == TM

    def latent_group(s):
        return pl.BlockSpec((1, TM, D), lambda b, i: (b, jnp.maximum(i * GROUPS + s - 1, 0), 0))

    return pl.pallas_call(
        _ffn_first_kernel,
        grid=(B, NTOK),
        in_specs=[pl.BlockSpec((1, CTX, D), lambda b, i: (b, 0, 0))] + [latent_group(s) for s in range(GROUPS)]
        + [_mod_spec(group), _const_spec((1, D)),
           _layer_spec((D, 2 * D_FF), layer), _layer_spec((D_FF, D), layer)],
        out_specs=_tok_spec(D),
        out_shape=jax.ShapeDtypeStruct((B, TT, D), F32),
        compiler_params=_cparams(("parallel", "arbitrary")),
        name="ffn1_first",
    )(ctx, *([x] * GROUPS), mod, g, w_in, w_out)


def _ffn_call(h, mod, g, w_in, w_out, layer, group, name):
    return pl.pallas_call(
        _ffn_kernel,
        grid=(B, NTOK),
        in_specs=[_tok_spec(D), _mod_spec(group), _const_spec((1, D)),
                  _layer_spec((D, 2 * D_FF), layer), _layer_spec((D_FF, D), layer)],
        out_specs=_tok_spec(D),
        out_shape=jax.ShapeDtypeStruct((B, TT, D), F32),
        compiler_params=_cparams(("parallel", "arbitrary")),
        name=name,
    )(h, mod, g, w_in, w_out)


def _group_ssq(x, ones_ref):
    sq = x * x
    hi = sq.astype(BF16)
    lo = (sq - hi.astype(F32)).astype(BF16)
    ones = ones_ref[...]
    tiles = [slice(c, c + MXU_TILE) for c in range(0, x.shape[1], MXU_TILE)]
    return jnp.concatenate([_dot(hi[:, t], ones) + _dot(lo[:, t], ones) for t in tiles], axis=1)


def _mixin_kernel(x_ref, mod_ref, g_ref, win_ref, qn_ref, wuq_ref, kvn_ref, wukv_ref,
                  mqg_ref, mkg_ref, nqg_ref, nkg_ref, rc_ref, rs_ref, ones_h_ref, ones_n_ref,
                  lx_ref, lg_ref, qm_ref, km_ref, vm_ref, nq_ref, nk_ref, nv_ref):
    x = x_ref[0]
    xb = _norm_mod(x, g_ref[...], mod_ref)[0].astype(BF16)
    y = _dot(xb, win_ref[...])
    lx_ref[0] = y[:, COL_LX:COL_LG]
    lg_ref[0] = y[:, COL_LG:COL_CQ]
    cq = y[:, COL_CQ:COL_CKV]
    ckv = y[:, COL_CKV:COL_KR]
    kr_slot = y[:, COL_KR:COL_NQ]
    rc, rs = rc_ref[...], rs_ref[...]
    n_slot = MLA_H * HEAD_SLOT

    q_raw = _dot((_rms(cq) * qn_ref[...]).astype(BF16), wuq_ref[...])
    inv_q = lax.rsqrt(_group_ssq(q_raw, ones_h_ref) * (1.0 / MLA_QK) + EPS)
    cos_q = rc * (mqg_ref[0:1] * MLA_QSCALE)
    sin_q = rs * (mqg_ref[1:2] * MLA_QSCALE)

    kv = _dot((_rms(ckv) * kvn_ref[...]).astype(BF16), wukv_ref[...])
    lane = lax.broadcasted_iota(jnp.int32, kr_slot.shape, 1)
    kr = jnp.where(lane < MLA_QK, kr_slot, 0.0)
    kr_partner = pltpu.roll(kr_slot, LANES - MLA_ROPE, 1)
    k_raw = kv[:, :n_slot] + jnp.concatenate([kr] * MLA_H, axis=1)
    inv_k = lax.rsqrt(_group_ssq(k_raw, ones_h_ref) * (1.0 / MLA_QK) + EPS)
    cos_k = rc * mkg_ref[0:1]
    sin_k = kr_partner * (rs * mkg_ref[1:2])
    q_heads = []
    for h in range(MLA_H):
        sl = slice(h * HEAD_SLOT, (h + 1) * HEAD_SLOT)
        qh = q_raw[:, sl] * cos_q + pltpu.roll(q_raw[:, sl], LANES - MLA_ROPE, 1) * sin_q
        q_heads.append(qh * inv_q[:, sl])
        km_ref[0, :, sl] = ((k_raw[:, sl] * cos_k + sin_k) * inv_k[:, sl]).astype(BF16)
    q_all = jnp.concatenate(q_heads, axis=1)
    for s in range(GROUPS):
        qm_ref[0, s] = q_all[s * TM:(s + 1) * TM].T.astype(BF16)
        vm_ref[0, s] = kv[s * TM:(s + 1) * TM, n_slot:].T.astype(BF16)

    nq = y[:, COL_NQ:COL_NK]
    nq = nq * lax.rsqrt(_group_ssq(nq, ones_n_ref) * (1.0 / NA_DH) + EPS) * (nqg_ref[...] * NA_QSCALE)
    nk = y[:, COL_NK:COL_NV]
    nk_ref[0] = (nk * lax.rsqrt(_group_ssq(nk, ones_n_ref) * (1.0 / NA_DH) + EPS) * nkg_ref[...]).astype(BF16)
    for s in range(GROUPS):
        nq_ref[0, s] = nq[s * TM:(s + 1) * TM].T.astype(BF16)
        nv_ref[0, s] = y[s * TM:(s + 1) * TM, COL_NV:IN_EXT].T.astype(BF16)


def _tile_t_spec(rows):
    return pl.BlockSpec((1, GROUPS, rows, TM), lambda b, i: (b, i, 0, 0))


def _mixin_call(h, mod, p, tabs):
    n_slot = MLA_H * HEAD_SLOT
    tok = lambda w, dt: (_tok_spec(w), jax.ShapeDtypeStruct((B, TT, w), dt))
    tile_t = lambda rows: (_tile_t_spec(rows), jax.ShapeDtypeStruct((B, NT, rows, TM), BF16))
    outs = [tok(LRU_W, F32), tok(LRU_W, F32), tile_t(n_slot), tok(n_slot, BF16), tile_t(MLA_H * MLA_V),
            tile_t(NA_W), tok(NA_W, BF16), tile_t(NA_W)]
    out_specs, out_shape = [o[0] for o in outs], [o[1] for o in outs]
    rope_spec = pl.BlockSpec((TOK, LANES), lambda b, i: (i, 0))
    return pl.pallas_call(
        _mixin_kernel,
        grid=(B, NTOK),
        in_specs=[_tok_spec(D), _mod_spec(1), _const_spec((1, D)), _const_spec((D, IN_EXT)),
                  _const_spec((1, MLA_QR)), _const_spec((MLA_QR, n_slot)),
                  _const_spec((1, MLA_KVR)), _const_spec((MLA_KVR, n_slot + MLA_H * MLA_V)),
                  _const_spec((2, HEAD_SLOT)), _const_spec((2, HEAD_SLOT)),
                  _const_spec((1, NA_W)), _const_spec((1, NA_W)),
                  rope_spec, rope_spec, _const_spec((MXU_TILE, MXU_TILE)), _const_spec((MXU_TILE, MXU_TILE))],
        out_specs=out_specs,
        out_shape=out_shape,
        compiler_params=_cparams(("parallel", "arbitrary")),
        name="mix_in",
    )(h, mod, p["norm_mix"], p["w_in"], p["q_norm"], p["w_uq"], p["kv_norm"], p["w_ukv"],
      p["mq_gain"], p["mk_gain"], p["nq_gain"], p["nk_gain"], *tabs)


def _dot_tn(a, b):
    return lax.dot_general(a, b, (((0,), (0,)), ((), ())), preferred_element_type=F32)


def _ffn_chunks(tiles):
    return tuple((lo, min(lo + tiles * MXU_TILE, D_FF)) for lo in range(0, D_FF, tiles * MXU_TILE))


POST_CHUNKS = _ffn_chunks(3)


def _post_kernel(h_ref, modm_ref, ylru_ref, ymla_ref, yna_ref, w1_ref, w2_ref, w3_ref,
                 modf_ref, g_ref, win_ref, wout_ref, o_ref):
    att = jnp.concatenate([_dot_tn(ymla_ref[0, s], w2_ref[...]) + _dot_tn(yna_ref[0, s], w3_ref[...])
                           for s in range(GROUPS)], axis=0)
    y = _dot(ylru_ref[0], w1_ref[...]) + att
    row = pl.program_id(1) * TOK + lax.broadcasted_iota(jnp.int32, (TOK, 1), 0)
    gate = jnp.where(row < CTX, modm_ref[0, 0, 0][2:3], modm_ref[0, 1, 0][2:3])
    h = h_ref[0] + gate * y
    o_ref[0] = _ffn_body(h, modf_ref, g_ref, win_ref, wout_ref, POST_CHUNKS)


def _post_call(h, mod, ylru, ymla, yna, p, g, w_in, w_out, layer):
    return pl.pallas_call(
        _post_kernel,
        grid=(B, NTOK),
        in_specs=[_tok_spec(D), _mod_spec(1), _tok_spec(LRU_W), _tile_t_spec(MLA_H * MLA_V), _tile_t_spec(NA_W),
                  _const_spec((LRU_W, D)), _const_spec((MLA_H * MLA_V, D)), _const_spec((NA_W, D)),
                  _mod_spec(2), _const_spec((1, D)),
                  _layer_spec((D, 2 * D_FF), layer), _layer_spec((D_FF, D), layer)],
        out_specs=_tok_spec(D),
        out_shape=jax.ShapeDtypeStruct((B, TT, D), F32),
        compiler_params=_cparams(("parallel", "arbitrary")),
        name="post",
    )(h, mod, ylru, ymla, yna, p["w_out_lru"], p["w_out_mla"], p["w_out_na"], mod, g, w_in, w_out)


LRU_CHUNK = TM
LRU_PAD = SUB
LRU_XROWS = TT + 3 * LRU_PAD


def _scan_block(a, u, carry, reverse):
    row = lax.broadcasted_iota(jnp.int32, a.shape, 0)
    for s in (1, 2, 4):
        if reverse:
            keep = row < SUB - s
            a_s = jnp.where(keep, pltpu.roll(a, SUB - s, 0), 1.0)
            u_s = jnp.where(keep, pltpu.roll(u, SUB - s, 0), 0.0)
        else:
            keep = row >= s
            a_s = jnp.where(keep, pltpu.roll(a, s, 0), 1.0)
            u_s = jnp.where(keep, pltpu.roll(u, s, 0), 0.0)
        u = a * u_s + u
        a = a * a_s
    h = a * carry + u
    edge = h[0:1] if reverse else h[SUB - 1:SUB]
    return h, jnp.broadcast_to(edge, a.shape)


LRU_SLICE = 64
LRU_NCHUNK = TT // LRU_CHUNK


def _lru_kernel(lx_ref, lg_ref, cw_ref, cb_ref, wg_ref, bg_ref, lam_ref, y_ref, xpad, hf, hb, au0, au1):
    zeros_pad = jnp.zeros((LRU_PAD, LANES), F32)
    xpad[0:LRU_PAD] = zeros_pad
    xpad[LRU_PAD:LRU_PAD + CTX] = lx_ref[0, 0:CTX]
    xpad[LRU_PAD + CTX:2 * LRU_PAD + CTX] = zeros_pad
    xpad[2 * LRU_PAD + CTX:2 * LRU_PAD + TT] = lx_ref[0, CTX:TT]
    xpad[2 * LRU_PAD + TT:3 * LRU_PAD + TT] = zeros_pad

    neg_c_softplus = -LRU_C * jax.nn.softplus(-lam_ref[...])
    n_slices = LRU_CHUNK // LRU_SLICE
    blocks_per_slice = LRU_SLICE // SUB

    def fwd_chunk(i):
        return i

    def bwd_chunk(i):
        return jnp.where(i == 0, 0, LRU_NCHUNK - i)

    def gates(c, d, s, buf):
        base = pl.multiple_of(jnp.where(c == 0, LRU_PAD, 2 * LRU_PAD + c * LRU_CHUNK), SUB) + s * LRU_SLICE
        xc = jnp.broadcast_to(cb_ref[...], (LRU_SLICE, LANES))
        for j in range(CONV_W):
            xc = xc + xpad[pl.ds(base + (j - CONV_LEFT), LRU_SLICE), :] * cw_ref[j:j + 1, :]
        cols = slice(d * 2 * LANES, (d + 1) * 2 * LANES)
        pre = _dot(xc.astype(BF16), wg_ref[0, :, cols]) + bg_ref[0, :, cols]
        r = jax.nn.sigmoid(pre[:, :LANES])
        gate_in = jax.nn.sigmoid(pre[:, LANES:])
        log_a = r * neg_c_softplus[d:d + 1]
        rows = slice(s * LRU_SLICE, (s + 1) * LRU_SLICE)
        buf[2 * d, rows, :] = jnp.exp(log_a)
        th = jnp.tanh(log_a)
        buf[2 * d + 1, rows, :] = jnp.sqrt(-2.0 * th / (1.0 - th)) * (gate_in * xc)

    def scan(c, d, s, buf, carry, dst):
        row0 = pl.multiple_of(c * LRU_CHUNK, SUB)
        for k in range(blocks_per_slice):
            blk = s * blocks_per_slice + k
            if d == 1:
                blk = LRU_CHUNK // SUB - 1 - blk
            rows = slice(blk * SUB, (blk + 1) * SUB)
            h, carry = _scan_block(buf[2 * d, rows, :], buf[2 * d + 1, rows, :], carry, d == 1)
            dst[pl.ds(row0 + blk * SUB, SUB), :] = h
        return carry

    def step(i, cur, nxt, carry):
        cf, cb = carry
        for s in range(n_slices):
            if nxt is not None:
                gates(fwd_chunk(i + 1), 0, s, nxt)
                gates(bwd_chunk(i + 1), 1, s, nxt)
            cf = scan(fwd_chunk(i), 0, s, cur, cf, hf)
            cb = scan(bwd_chunk(i), 1, s, cur, cb, hb)
        return cf, cb

    for s in range(n_slices):
        gates(0, 0, s, au0)
        gates(0, 1, s, au0)

    def pair(j, carry):
        carry = step(2 * j, au0, au1, carry)
        return step(2 * j + 1, au1, au0, carry)

    assert LRU_NCHUNK % 2 == 1
    zero = jnp.zeros((SUB, LANES), F32)
    carry = lax.fori_loop(0, LRU_NCHUNK // 2, pair, (zero, zero))
    step(LRU_NCHUNK - 1, au0, None, carry)

    def finish(c, _):
        rows = pl.ds(pl.multiple_of(c * LRU_CHUNK, LRU_CHUNK), LRU_CHUNK)
        y_ref[0, rows, :] = ((hf[rows, :] + hb[rows, :]) * jax.nn.gelu(lg_ref[0, rows, :])).astype(y_ref.dtype)
        return 0

    lax.fori_loop(0, LRU_NCHUNK, finish, 0)


def _lru_call(lx, lg, p):
    nl = LRU_W // LANES
    seq_spec = pl.BlockSpec((1, TT, LANES), lambda b, j: (b, 0, j))
    return pl.pallas_call(
        _lru_kernel,
        grid=(B, nl),
        in_specs=[seq_spec, seq_spec,
                  pl.BlockSpec((CONV_W, LANES), lambda b, j: (0, j)),
                  pl.BlockSpec((1, LANES), lambda b, j: (0, j)),
                  pl.BlockSpec((1, LANES, 4 * LANES), lambda b, j: (j, 0, 0)),
                  pl.BlockSpec((1, 1, 4 * LANES), lambda b, j: (j, 0, 0)),
                  pl.BlockSpec((2, LANES), lambda b, j: (0, j))],
        out_specs=seq_spec,
        out_shape=jax.ShapeDtypeStruct((B, TT, LRU_W), BF16),
        scratch_shapes=[pltpu.VMEM((LRU_XROWS, LANES), F32), pltpu.VMEM((TT, LANES), F32),
                        pltpu.VMEM((TT, LANES), F32),
                        pltpu.VMEM((4, LRU_CHUNK, LANES), F32), pltpu.VMEM((4, LRU_CHUNK, LANES), F32)],
        compiler_params=_cparams(("parallel", "arbitrary")),
        name="lru",
    )(lx, lg, p["conv_w"], p["conv_b"], p["lru_wg"], p["lru_bg"], p["lru_lam"])


MLA_TQ = 2048
MLA_TK = TM
MLA_NKV = TT // MLA_TK
MLA_UNROLL = 8
MLA_QCOL = MXU_TILE
MLA_PV_LAG = 1


def _with_ones(vt):
    return jnp.concatenate([vt, jnp.ones((V_ROWS - vt.shape[0], vt.shape[1]), vt.dtype)], axis=0)


def _softmax_pv(s, vt, m, acc):
    m_new = jnp.maximum(m, jnp.max(s, axis=0, keepdims=True))
    p = jnp.exp2(s - m_new).astype(BF16)
    return m_new, acc * jnp.exp2(m - m_new) + _dot(vt, p)


MLA_NCOL = MLA_TQ // MLA_QCOL
MLA_COLS = [slice(c * MLA_QCOL, (c + 1) * MLA_QCOL) for c in range(MLA_NCOL)]


def _attend(q_cols, q_next_cols, k_ref, vt_ref, cur, nxt):
    tk = MLA_TK
    assert MLA_NKV % MLA_UNROLL == 1 and MLA_UNROLL % 2 == 0

    def keys(i):
        return k_ref[0, pl.ds(pl.multiple_of(i * tk, tk), tk), :]

    def values(i):
        return _with_ones(vt_ref[0, i])

    def group(j, carry):
        ms, accs = list(carry[0]), list(carry[1])
        a, b = cur, nxt
        pending = []

        def flush(keep):
            while len(pending) > keep:
                pc, pp, pa, pv = pending.pop(0)
                accs[pc] = accs[pc] * pa + _dot(pv, pp)

        for u in range(MLA_UNROLL):
            i = MLA_UNROLL * j + u
            k_next, v_cur = keys(i + 1), values(i)
            for c, sl in enumerate(MLA_COLS):
                b[:, sl] = _dot(k_next, q_cols[c])
                s = a[:, sl]
                m_new = jnp.maximum(ms[c], jnp.max(s, axis=0, keepdims=True))
                pending.append((c, jnp.exp2(s - m_new).astype(BF16), jnp.exp2(ms[c] - m_new), v_cur))
                ms[c] = m_new
                flush(MLA_PV_LAG)
            a, b = b, a
        flush(0)
        return tuple(ms), tuple(accs)

    init = (tuple(jnp.full((1, MLA_QCOL), NEG, F32) for _ in MLA_COLS),
            tuple(jnp.zeros((V_ROWS, MLA_QCOL), F32) for _ in MLA_COLS))
    ms, accs = lax.fori_loop(0, MLA_NKV // MLA_UNROLL, group, init)
    k_first, v_last = k_ref[0, 0:tk, :], values(MLA_NKV - 1)
    outs = []
    for c, sl in enumerate(MLA_COLS):
        nxt[:, sl] = _dot(k_first, q_next_cols[c])
        _, acc = _softmax_pv(cur[:, sl], v_last, ms[c], accs[c])
        outs.append(acc[:MLA_V] / acc[MLA_V:MLA_V + 1])
    return jnp.concatenate(outs, axis=1)


def _mla_kernel(q_ref, k_ref, vt_ref, o_ref, s_a, s_b):
    sc = _dot(k_ref[0, 0:CTX, :], q_ref[0, 0])
    _, acc = _softmax_pv(sc, _with_ones(vt_ref[0, 0]), jnp.full((1, CTX), NEG, F32),
                         jnp.zeros((V_ROWS, CTX), F32))
    o_ref[0, 0] = (acc[:MLA_V] / acc[MLA_V:MLA_V + 1]).astype(o_ref.dtype)

    n_tiles = SEQ // MLA_TQ
    assert n_tiles % 2 == 0 and MLA_QCOL == TM

    def q_tile(t):
        t = jnp.minimum(t, n_tiles - 1)
        return [q_ref[0, 1 + t * MLA_NCOL + c] for c in range(MLA_NCOL)]

    def store(t, o):
        for g in range(MLA_NCOL):
            o_ref[0, 1 + t * MLA_NCOL + g] = o[:, g * TM:(g + 1) * TM].astype(o_ref.dtype)

    q0 = q_tile(0)
    for c, sl in enumerate(MLA_COLS):
        s_a[:, sl] = _dot(k_ref[0, 0:MLA_TK, :], q0[c])

    def tile_pair(tt, _):
        t = 2 * tt
        store(t, _attend(q_tile(t), q_tile(t + 1), k_ref, vt_ref, s_a, s_b))
        store(t + 1, _attend(q_tile(t + 1), q_tile(t + 2), k_ref, vt_ref, s_b, s_a))
        return 0

    lax.fori_loop(0, n_tiles // 2, tile_pair, 0)


def _mla_call(qm, km, vt):
    assert MLA_TK == TM
    head_t_spec = pl.BlockSpec((1, NT, MLA_V, TM), lambda b, h: (b, 0, h, 0))
    return pl.pallas_call(
        _mla_kernel,
        grid=(B, MLA_H),
        in_specs=[pl.BlockSpec((1, NT, HEAD_SLOT, TM), lambda b, h: (b, 0, h, 0)),
                  pl.BlockSpec((1, TT, HEAD_SLOT), lambda b, h: (b, 0, h)),
                  head_t_spec],
        out_specs=head_t_spec,
        out_shape=jax.ShapeDtypeStruct((B, NT, MLA_H * MLA_V, TM), BF16),
        scratch_shapes=[pltpu.VMEM((MLA_TK, MLA_TQ), F32), pltpu.VMEM((MLA_TK, MLA_TQ), F32)],
        compiler_params=_cparams(("parallel", "arbitrary")),
        name="mla_attn",
    )(qm, km, vt)


def _na_kernel(q_ref, k_ref, vt_ref, tab_ref, o_ref):
    step = pl.program_id(1)
    row_head = lax.broadcasted_iota(jnp.int32, (NA_W, TM), 0) // NA_DH

    def head_queries(g):
        q_all = q_ref[0, g]
        return [jnp.where(row_head == h, q_all, jnp.zeros_like(q_all)) for h in range(NA_H)]

    def values(g, h):
        return vt_ref[0, g, h * NA_DH:(h + 1) * NA_DH, :]

    def attend(g, h, p, vt):
        acc = _dot(_with_ones(vt), p)
        o_ref[0, g, h * NA_DH:(h + 1) * NA_DH, :] = (acc[:NA_DH] / acc[NA_DH:NA_DH + 1]).astype(o_ref.dtype)

    def softmax(s):
        return jnp.exp2(s - jnp.max(s, axis=0, keepdims=True)).astype(BF16)

    @pl.when(step == 0)
    def _():
        q_heads = head_queries(0)
        for h in range(NA_H):
            attend(0, h, softmax(_dot(k_ref[0, 0], q_heads[h])), values(0, h))

    @pl.when(step > 0)
    def _():
        items = []
        for u in range(NA_STEP_GROUPS):
            jj = (step - 1) * NA_STEP_GROUPS + u
            g0 = jnp.clip(jj - 1, 0, NA_NGROUPS - NA_BAND_G)
            variant = jnp.where(jj == 0, 0, jnp.where(jj == NA_NGROUPS - 1, 2, 1))
            band = k_ref[0, pl.ds(1 + g0, NA_BAND_G)].reshape(NA_BAND_G * TM, NA_W)
            keys = jnp.concatenate([k_ref[0, 0], band], axis=0)
            q_heads = head_queries(1 + jj)
            items += [(1 + jj, g0, variant, keys, q_heads[h], h) for h in range(NA_H)]
        scores, probs = {}, {}
        for t in range(len(items) + 2):
            if t < len(items):
                scores[t] = _dot(items[t][3], items[t][4])
            if 0 <= t - 1 < len(items):
                _, _, variant, _, _, h = items[t - 1]
                s = scores.pop(t - 1)
                probs[t - 1] = softmax(jnp.concatenate([s[:TM], s[TM:] + tab_ref[variant, h]], axis=0))
            if 0 <= t - 2 < len(items):
                g, g0, _, _, _, h = items[t - 2]
                vt = jnp.concatenate([values(0, h)] + [values(1 + g0 + i, h) for i in range(NA_BAND_G)], axis=1)
                attend(g, h, probs.pop(t - 2), vt)


NA_STEP_GROUPS = 4


def _na_call(q, k, vt, tab, layer):
    assert NA_NGROUPS % NA_STEP_GROUPS == 0
    whole = pl.BlockSpec((1, NT, TM, NA_W), lambda b, j: (b, 0, 0, 0))
    return pl.pallas_call(
        _na_kernel,
        grid=(B, 1 + NA_NGROUPS // NA_STEP_GROUPS),
        in_specs=[whole, whole, whole, _layer_spec((3, NA_H, NA_BAND_G * TM, TM), layer)],
        out_specs=whole,
        out_shape=jax.ShapeDtypeStruct((B, NT, NA_W, TM), BF16),
        compiler_params=_cparams(("parallel", "arbitrary")),
        name="na_attn",
    )(q, k, vt, tab)


def _rope_tables():
    t = jnp.arange(SEQ)
    pos = jnp.stack([t // GRID_W, t % GRID_W], axis=-1).astype(F32)
    half = MLA_ROPE // 2
    inv = ROPE_BASE ** (-jnp.arange(0, half, 2, dtype=F32) / half)
    ang = pos[:, :, None] * inv
    cos, sin = jnp.cos(ang), jnp.sin(ang)
    ones = jnp.ones((SEQ, MLA_NOPE), F32)
    c = jnp.concatenate([ones, cos[:, 0], cos[:, 0], cos[:, 1], cos[:, 1], ones[:, :32]], axis=-1)
    s = jnp.concatenate([0 * ones, -sin[:, 0], sin[:, 0], -sin[:, 1], sin[:, 1], 0 * ones[:, :32]], axis=-1)
    ctx_c = jnp.ones((CTX, LANES), F32)
    ctx_s = jnp.zeros((CTX, LANES), F32)
    return jnp.concatenate([ctx_c, c]), jnp.concatenate([ctx_s, s])


ROPE_PARTNER = np.concatenate([np.arange(8, 16), np.arange(0, 8), np.arange(24, 32), np.arange(16, 24)])


def _group_ones(width, group, used):
    lane = np.arange(width)
    g = lane // group
    return jnp.asarray((g[:, None] == g[None, :]) & (lane[:, None] % group < used), BF16)


def _na_window_pattern():
    n_band = NA_BAND_G * NA_QROWS
    i = np.arange(n_band)[:, None]
    a = np.arange(NA_QROWS)[None, :]

    def pattern(jj):
        g0 = min(max(jj - 1, 0), NA_NGROUPS - NA_BAND_G)
        krow = NA_QROWS * g0 + i
        qrow = NA_QROWS * jj + a
        r0q = np.clip(qrow - NA_WR // 2, 0, GRID_H - NA_WR)
        return (krow >= r0q) & (krow < r0q + NA_WR), krow - qrow + NA_WR - 1

    interior = [pattern(jj) for jj in range(1, NA_NGROUPS - 1)]
    assert all((v == interior[0][0]).all() and (r[v] == interior[0][1][v]).all() for v, r in interior)
    pats = [pattern(0), interior[0], pattern(NA_NGROUPS - 1)]
    valid = np.stack([p[0] for p in pats])
    row_off = np.stack([p[1] for p in pats])
    assert (row_off[valid] >= 0).all() and (row_off[valid] <= 2 * NA_WR - 2).all()
    return valid, row_off


def _na_tables(rpb):
    kc = jnp.arange(GRID_W)[:, None]
    qc = jnp.arange(GRID_W)[None, :]
    win = jnp.clip(qc - NA_WC // 2, 0, GRID_W - NA_WC)
    ok_col = (kc >= win) & (kc < win + NA_WC)
    col_off = kc - qc + NA_WC - 1
    pick_col = ((col_off[None] == jnp.arange(2 * NA_WC - 1)[:, None, None]) & ok_col[None]).astype(F32)
    valid, row_off = _na_window_pattern()
    pick_row = ((row_off[..., None] == np.arange(2 * NA_WR - 1)) & valid[..., None]).astype(np.float32)
    eye_a = jnp.eye(NA_QROWS, dtype=F32)
    pick_col = (eye_a[:, None, None, :, None] * pick_col[None, :, :, None, :]).reshape(
        NA_QROWS, 2 * NA_WC - 1, GRID_W, TM)
    rows = jnp.einsum("vian,dhnj->dvhiaj", pick_row, rpb, precision=lax.Precision.HIGHEST)
    tab = jnp.einsum("dvhiaj,ajkz->dvhikz", rows, pick_col, precision=lax.Precision.HIGHEST)
    keep = (valid[:, None, :, None, :, None] & ok_col[None, None, None, :, None, :]).reshape(
        3, 1, NA_BAND_G * NA_QROWS, GRID_W, TM)
    tab = jnp.where(keep, tab * math.log2(math.e), NEG)
    return tab.reshape(DEPTH, 3, NA_H, NA_BAND_G * TM, TM)


def _block_diag(w):
    w = w.reshape(DEPTH, 2, LRU_W // LANES, 2, LRU_BW, LRU_BW)
    z = jnp.zeros_like(w[:, :, :, 0])
    top = jnp.concatenate([w[:, :, :, 0], z], axis=-1)
    bot = jnp.concatenate([z, w[:, :, :, 1]], axis=-1)
    return jnp.concatenate([top, bot], axis=-2)


def _prepare(w_in, w_out, lru_conv_b, lru_w_a, lru_b_a, lru_w_x, lru_b_x, mla_w_uq, mla_w_ukv,
             mla_q_gain, mla_k_gain, na_q_gain, na_k_gain):
    z = lambda n: jnp.zeros((DEPTH, D, n), F32)
    kr = w_in[..., COL_KR:COL_KR + MLA_ROPE]
    w_in_ext = jnp.concatenate([w_in[..., :COL_KR], z(MLA_NOPE), kr, kr[..., ROPE_PARTNER],
                                w_in[..., COL_KR + MLA_ROPE:]], axis=-1).astype(BF16)
    assert w_in_ext.shape[-1] == IN_EXT
    w_uq = mla_w_uq.reshape(DEPTH, MLA_QR, MLA_H, MLA_QK)
    w_uq = jnp.concatenate([w_uq, w_uq[..., MLA_NOPE + ROPE_PARTNER]], axis=-1).reshape(DEPTH, MLA_QR, -1)
    w_ukv = mla_w_ukv.reshape(DEPTH, MLA_KVR, MLA_H, MLA_NOPE + MLA_V)
    w_uk = jnp.pad(w_ukv[..., :MLA_NOPE], ((0, 0), (0, 0), (0, 0), (0, HEAD_SLOT - MLA_NOPE)))
    w_ukv_ext = jnp.concatenate([w_uk.reshape(DEPTH, MLA_KVR, -1),
                                 w_ukv[..., MLA_NOPE:].reshape(DEPTH, MLA_KVR, -1)], axis=-1)

    def pad_gain(g):
        partner = jnp.concatenate([jnp.zeros((DEPTH, MLA_NOPE), F32), g[:, MLA_NOPE + ROPE_PARTNER]], axis=-1)
        both = jnp.stack([g, partner], axis=1)
        return jnp.pad(both, ((0, 0), (0, 0), (0, HEAD_SLOT - MLA_QK)))
    tile_gain = lambda g: jnp.tile(g, (1, NA_H)).reshape(DEPTH, 1, NA_W)
    wa, wx = _block_diag(lru_w_a), _block_diag(lru_w_x)
    wg = jnp.concatenate([wa[:, 0], wx[:, 0], wa[:, 1], wx[:, 1]], axis=-1).astype(BF16)
    ba = lru_b_a.reshape(DEPTH, 2, LRU_W // LANES, 1, LANES)
    bx = lru_b_x.reshape(DEPTH, 2, LRU_W // LANES, 1, LANES)
    bg = jnp.concatenate([ba[:, 0], bx[:, 0], ba[:, 1], bx[:, 1]], axis=-1)
    w_out = w_out.astype(BF16)
    return dict(
        w_in=w_in_ext, w_uq=w_uq.astype(BF16), w_ukv=w_ukv_ext.astype(BF16),
        mq_gain=pad_gain(mla_q_gain), mk_gain=pad_gain(mla_k_gain),
        nq_gain=tile_gain(na_q_gain), nk_gain=tile_gain(na_k_gain),
        lru_wg=wg, lru_bg=bg, conv_b=lru_conv_b.reshape(DEPTH, 1, LRU_W),
        w_out_lru=w_out[:, :LRU_W], w_out_mla=w_out[:, LRU_W:LRU_W + MLA_H * MLA_V],
        w_out_na=w_out[:, LRU_W + MLA_H * MLA_V:],
    )


def kernel(x, c, ctx, c_ctx, w_mod, b_mod, norm_ffn1, ffn1_w_in, ffn1_w_out, norm_mix, w_in, w_out, lru_conv_w, lru_conv_b, lru_w_a, lru_b_a, lru_w_x, lru_b_x, lru_lambda, mla_q_norm, mla_w_uq, mla_kv_norm, mla_w_ukv, mla_q_gain, mla_k_gain, na_q_gain, na_k_gain, na_rpb, norm_ffn2, ffn2_w_in, ffn2_w_out):
    cc = jnp.concatenate([c, c_ctx[None], jnp.zeros((SUB - B - 1, D), F32)], axis=0)
    mod = _mod_call(cc, w_mod, b_mod)
    mod = mod.reshape(DEPTH, SUB, 3, 3, D)
    mod = jnp.stack([jnp.broadcast_to(mod[:, B:B + 1], (DEPTH, B, 3, 3, D)), mod[:, :B]], axis=2)

    prm = _prepare(w_in, w_out, lru_conv_b, lru_w_a, lru_b_a, lru_w_x, lru_b_x, mla_w_uq, mla_w_ukv,
                   mla_q_gain, mla_k_gain, na_q_gain, na_k_gain)
    prm.update(norm_mix=norm_mix.reshape(DEPTH, 1, D), q_norm=mla_q_norm.reshape(DEPTH, 1, MLA_QR),
               kv_norm=mla_kv_norm.reshape(DEPTH, 1, MLA_KVR), conv_w=lru_conv_w, lru_lam=lru_lambda)
    f1_in, f1_out = ffn1_w_in.astype(BF16), ffn1_w_out.astype(BF16)
    f2_in, f2_out = ffn2_w_in.astype(BF16), ffn2_w_out.astype(BF16)
    g1, g2 = norm_ffn1.reshape(DEPTH, 1, D), norm_ffn2.reshape(DEPTH, 1, D)
    rope_tabs = _rope_tables() + (_group_ones(MXU_TILE, HEAD_SLOT, MLA_QK), _group_ones(MXU_TILE, NA_DH, NA_DH))
    na_tabs = _na_tables(na_rpb)

    h = None
    for l in range(DEPTH):
        p = {k: v[l] for k, v in prm.items()}
        if l == 0:
            h = _ffn_first_call(ctx, x, mod[l], g1[l], f1_in, f1_out, l, 0)
        else:
            h = _ffn_call(h, mod[l], g1[l], f1_in, f1_out, l, 0, "ffn1")
        lx, lg, qm, km, vm, nq, nk, nv = _mixin_call(h, mod[l], p, rope_tabs)
        ylru = _lru_call(lx, lg, p)
        ymla = _mla_call(qm, km, vm)
        yna = _na_call(nq, nk.reshape(B, NT, TM, NA_W), nv, na_tabs, l)
        h = _post_call(h, mod[l], ylru, ymla, yna, p, g2[l], f2_in, f2_out, l)
    return h[:, CTX:]
```
